```python
import math
import jax, jax.numpy as jnp
from jax import lax
import numpy as np

D_MODEL = 1024
BATCH = 16
SEQ = 4096
DEPTH = 1

N_META = 16
D_RNN = D_MODEL
N_RNN_BLOCKS = 8
RNN_BLOCK = D_RNN // N_RNN_BLOCKS
CONV_W = 4
LRU_C = 8.0
N_HEADS = 8
HEAD_DIM = D_MODEL // (2 * N_HEADS)
V_DIM = 2 * HEAD_DIM
ATTN_WIDTH = N_HEADS * V_DIM
QK_WIDTH = N_HEADS * 2 * HEAD_DIM
ROPE_THETA = 10000.0
Q_BLOCK = 128
D_FF = 4 * D_MODEL
EPS = 1e-6
IN_COLS = 2 * D_RNN + 2 * QK_WIDTH + ATTN_WIDTH + 2 * D_MODEL

kernel_name = "hybrid_rglru_diffattn_gated_block"


def _rmsnorm(x, g):
    xf = x.astype(jnp.float32)
    y = xf * lax.rsqrt(jnp.mean(xf * xf, axis=-1, keepdims=True) + EPS)
    return (y * g.astype(jnp.float32)).astype(x.dtype)


def _rope_tables(n_pos):
    inv = 1.0 / (ROPE_THETA ** (jnp.arange(0, HEAD_DIM, 2, dtype=jnp.float32) / HEAD_DIM))
    ang = jnp.arange(n_pos, dtype=jnp.float32)[:, None] * inv[None, :]
    return jnp.cos(ang), jnp.sin(ang)


def _rope(x, cos, sin):
    xf = x.astype(jnp.float32)
    x1, x2 = jnp.split(xf, 2, axis=-1)
    c = cos[None, :, None, None, :]
    s = sin[None, :, None, None, :]
    return jnp.concatenate([x1 * c - x2 * s, x2 * c + x1 * s], axis=-1).astype(x.dtype)


def _rglru_branch(xr, gr, conv_w, conv_b, w_a, b_a, w_x, b_x, lru_lambda):
    B, T, _ = xr.shape
    xf = xr.astype(jnp.float32)
    xpad = jnp.pad(xf, ((0, 0), (CONV_W - 1, 0), (0, 0)))
    xc = conv_b.astype(jnp.float32)
    for k in range(CONV_W):
        xc = xc + conv_w[k].astype(jnp.float32) * xpad[:, k:k + T]
    xb = xc.reshape(B, T, N_RNN_BLOCKS, RNN_BLOCK)
    r = jax.nn.sigmoid(jnp.einsum('btnc,ncd->btnd', xb, w_a.astype(jnp.float32)) + b_a.astype(jnp.float32))
    i = jax.nn.sigmoid(jnp.einsum('btnc,ncd->btnd', xb, w_x.astype(jnp.float32)) + b_x.astype(jnp.float32))
    r = r.reshape(B, T, D_RNN)
    i = i.reshape(B, T, D_RNN)
    log_a = -LRU_C * r * jax.nn.softplus(-lru_lambda.astype(jnp.float32))
    a = jnp.exp(log_a)
    u = jnp.sqrt(-jnp.expm1(2.0 * log_a)) * (i * xc)

    def step(h, inp):
        a_t, u_t = inp
        h = a_t * h + u_t
        return h, h

    h0 = jnp.zeros((B, D_RNN), jnp.float32)
    _, hs = lax.scan(step, h0, (jnp.swapaxes(a, 0, 1), jnp.swapaxes(u, 0, 1)))
    h = jnp.swapaxes(hs, 0, 1)
    y = jax.nn.gelu(gr.astype(jnp.float32)) * h
    return y.astype(xr.dtype)


def _diff_attn_block(q, k, v, q_pos, k_pos, lam):
    scale = 1.0 / math.sqrt(HEAD_DIM)
    s = jnp.einsum('bqhcd,bkhcd->bhcqk', q.astype(jnp.float32), k.astype(jnp.float32)) * scale
    mask = k_pos[None, :] <= q_pos[:, None]
    s = jnp.where(mask[None, None, None], s, jnp.finfo(jnp.float32).min)
    p = jax.nn.softmax(s, axis=-1)
    p_diff = p[:, :, 0] - lam * p[:, :, 1]
    return jnp.einsum('bhqk,bkhe->bqhe', p_diff, v.astype(jnp.float32))


def _diff_attention(q, k, v, lam, lam_init, g_subln, n_real):
    B, T = q.shape[0], q.shape[1]
    pos = jnp.arange(T, dtype=jnp.int32)
    outs = [_diff_attn_block(q[:, :N_META], k[:, :N_META], v[:, :N_META],
                             pos[:N_META], pos[:N_META], lam)]
    for bi in range(n_real // Q_BLOCK):
        start = N_META + bi * Q_BLOCK
        end = start + Q_BLOCK
        outs.append(_diff_attn_block(q[:, start:end], k[:, :end], v[:, :end],
                                     pos[start:end], pos[:end], lam))
    o = jnp.concatenate(outs, axis=1)
    o = o * lax.rsqrt(jnp.mean(o * o, axis=-1, keepdims=True) + EPS) * g_subln.astype(jnp.float32)
    o = o * (1.0 - lam_init)
    return o.reshape(B, T, ATTN_WIDTH).astype(q.dtype)


def setup_inputs(seed: int = 0) -> dict:
    key = jax.random.key(seed)
    ks = jax.random.split(key, 24)
    f32 = jnp.float32
    nrm = lambda k, shape, s: jax.random.normal(k, shape, f32) * s
    u = jax.random.uniform(ks[10], (DEPTH, D_RNN), f32, minval=0.9, maxval=0.999)
    base = u ** (1.0 / LRU_C)
    lru_lambda = jnp.log(base) - jnp.log1p(-base)
    return {
        "x": nrm(ks[0], (BATCH, SEQ, D_MODEL), 1.0),
        "meta_tokens": nrm(ks[1], (N_META, D_MODEL), 1.0),
        "g_mix": 1.0 + nrm(ks[2], (DEPTH, D_MODEL), 0.01),
        "w_in": nrm(ks[3], (DEPTH, D_MODEL, IN_COLS), D_MODEL ** -0.5),
        "conv_w": nrm(ks[4], (DEPTH, CONV_W, D_RNN), CONV_W ** -0.5),
        "conv_b": nrm(ks[5], (DEPTH, D_RNN), 0.01),
        "w_a": nrm(ks[6], (DEPTH, N_RNN_BLOCKS, RNN_BLOCK, RNN_BLOCK), RNN_BLOCK ** -0.5),
        "b_a": nrm(ks[7], (DEPTH, N_RNN_BLOCKS, RNN_BLOCK), 0.01),
        "w_x": nrm(ks[8], (DEPTH, N_RNN_BLOCKS, RNN_BLOCK, RNN_BLOCK), RNN_BLOCK ** -0.5),
        "b_x": nrm(ks[9], (DEPTH, N_RNN_BLOCKS, RNN_BLOCK), 0.01),
        "lru_lambda": lru_lambda,
        "lam_q1": nrm(ks[11], (DEPTH, HEAD_DIM), 0.1),
        "lam_k1": nrm(ks[12], (DEPTH, HEAD_DIM), 0.1),
        "lam_q2": nrm(ks[13], (DEPTH, HEAD_DIM), 0.1),
        "lam_k2": nrm(ks[14], (DEPTH, HEAD_DIM), 0.1),
        "g_subln": 1.0 + nrm(ks[15], (DEPTH, V_DIM), 0.01),
        "w_rnn_out": nrm(ks[16], (DEPTH, D_RNN, D_MODEL), D_RNN ** -0.5),
        "w_attn_out": nrm(ks[17], (DEPTH, ATTN_WIDTH, D_MODEL), ATTN_WIDTH ** -0.5),
        "w_o": nrm(ks[18], (DEPTH, D_MODEL, D_MODEL), D_MODEL ** -0.5),
        "g_mlp": 1.0 + nrm(ks[19], (DEPTH, D_MODEL), 0.01),
        "w_ff1": nrm(ks[20], (DEPTH, D_MODEL, D_FF), D_MODEL ** -0.5),
        "w_ff2": nrm(ks[21], (DEPTH, D_FF, D_MODEL), D_FF ** -0.5),
        "g_final": 1.0 + nrm(ks[22], (D_MODEL,), 0.01),
    }


def reference(x, meta_tokens, g_mix, w_in, conv_w, conv_b, w_a, b_a, w_x, b_x, lru_lambda,
              lam_q1, lam_k1, lam_q2, lam_k2, g_subln, w_rnn_out, w_attn_out, w_o,
              g_mlp, w_ff1, w_ff2, g_final):
    B, S, _ = x.shape
    meta = jnp.broadcast_to(meta_tokens.astype(x.dtype)[None], (B, N_META, D_MODEL))
    h = jnp.concatenate([meta, x], axis=1)
    T = h.shape[1]
    cos, sin = _rope_tables(T)
    c0 = D_RNN
    c1 = c0 + D_RNN
    c2 = c1 + QK_WIDTH
    c3 = c2 + QK_WIDTH
    c4 = c3 + ATTN_WIDTH
    c5 = c4 + D_MODEL
    for l in range(DEPTH):
        hn = _rmsnorm(h, g_mix[l])
        proj = jnp.einsum('btd,dc->btc', hn, w_in[l])
        xr, gr = proj[..., :c0], proj[..., c0:c1]
        q = proj[..., c1:c2].reshape(B, T, N_HEADS, 2, HEAD_DIM)
        k = proj[..., c2:c3].reshape(B, T, N_HEADS, 2, HEAD_DIM)
        v = proj[..., c3:c4].reshape(B, T, N_HEADS, V_DIM)
        gate_r, gate_a = proj[..., c4:c5], proj[..., c5:]

        y_r = _rglru_branch(xr, gr, conv_w[l], conv_b[l], w_a[l], b_a[l], w_x[l], b_x[l], lru_lambda[l])
        y_r = jnp.einsum('btc,cd->btd', y_r, w_rnn_out[l])

        q = _rope(q, cos, sin)
        k = _rope(k, cos, sin)
        lam_init = 0.8 - 0.6 * math.exp(-0.3 * l)
        lam = (jnp.exp(jnp.sum(lam_q1[l].astype(jnp.float32) * lam_k1[l].astype(jnp.float32)))
               - jnp.exp(jnp.sum(lam_q2[l].astype(jnp.float32) * lam_k2[l].astype(jnp.float32)))
               + lam_init)
        y_a = _diff_attention(q, k, v, lam, lam_init, g_subln[l], S)
        y_a = jnp.einsum('btc,cd->btd', y_a, w_attn_out[l])

        merged = jax.nn.sigmoid(gate_r) * y_r + jax.nn.sigmoid(gate_a) * y_a
        h = h + jnp.einsum('btd,de->bte', merged, w_o[l])

        hn2 = _rmsnorm(h, g_mlp[l])
        ff = jnp.square(jax.nn.relu(jnp.einsum('btd,df->btf', hn2, w_ff1[l])))
        h = h + jnp.einsum('btf,fd->btd', ff, w_ff2[l])
    out = _rmsnorm(h, g_final)
    return out[:, N_META:]
```

```python
import functools
import math

import jax
import jax.numpy as jnp
from jax import lax
from jax.experimental import pallas as pl
from jax.experimental.pallas import tpu as pltpu

D_MODEL = 1024
N_META = 16
N_RNN_BLOCKS = 8
RNN_BLOCK = D_MODEL // N_RNN_BLOCKS
CONV_W = 4
LRU_C = 8.0
N_HEADS = 8
HEAD_DIM = 64
V_DIM = 128
ROPE_THETA = 10000.0
D_FF = 4 * D_MODEL
EPS = 1e-6
N_IN_GROUPS = 7
LAM_INIT = 0.8 - 0.6 * math.exp(-0.3 * 0)

LANES = 128
SUBLANES = 8
VMEM_LIMIT = 56 * 1024 * 1024
MASK_VALUE = -1e30

BF16 = jnp.bfloat16
F32 = jnp.float32


def _sigmoid(x):
    return 0.5 * jnp.tanh(0.5 * x) + 0.5


def _gelu_tanh(x):
    c = math.sqrt(2.0 / math.pi)
    return x * (0.5 * (1.0 + jnp.tanh(c * (x + 0.044715 * (x * x * x)))))


def _rms_scale(xf):
    return lax.rsqrt(jnp.mean(xf * xf, axis=-1, keepdims=True) + EPS)


def _resident(shape):
    nd = len(shape)
    return pl.BlockSpec(shape, lambda *_: (0,) * nd, pipeline_mode=pl.Buffered(1))


def _inproj_kernel(x_ref, g_ref, w_ref, cos_ref, sin_ref,
                   xr_ref, gr_ref, q_ref, k_ref, v_ref, sgr_ref, sga_ref, *, q_scale):
    xf = x_ref[...]
    hn = (xf * _rms_scale(xf) * g_ref[...]).astype(BF16)
    cos = cos_ref[...]
    sin = sin_ref[...]
    lane = lax.broadcasted_iota(jnp.int32, (1, LANES), 1)
    first_half = (lane & (HEAD_DIM // 2)) == 0

    def proj(c):
        return jnp.dot(hn, w_ref[:, c * D_MODEL:(c + 1) * D_MODEL],
                       preferred_element_type=F32)

    def rope_store(r, out_ref, scale):
        for j in range(D_MODEL // LANES):
            xg = r[:, j * LANES:(j + 1) * LANES]
            rot = jnp.where(first_half,
                            pltpu.roll(xg, LANES - HEAD_DIM // 2, 1),
                            pltpu.roll(xg, HEAD_DIM // 2, 1))
            o = xg * cos + rot * sin
            if scale != 1.0:
                o = o * scale
            out_ref[:, j * LANES:(j + 1) * LANES] = o.astype(BF16)

    xr_ref[...] = proj(0).astype(BF16)
    gr_ref[...] = proj(1).astype(BF16)
    rope_store(proj(2), q_ref, q_scale)
    rope_store(proj(3), k_ref, 1.0)
    v_ref[...] = proj(4).astype(BF16)
    sgr_ref[...] = _sigmoid(proj(5)).astype(BF16)
    sga_ref[...] = _sigmoid(proj(6)).astype(BF16)


def _inproj(x2d, g, w_in, cos, sin, *, tm, pos_tiles, q_scale):
    n = x2d.shape[0]
    row_spec = pl.BlockSpec((tm, D_MODEL), lambda i: (i, 0))
    tab_spec = pl.BlockSpec((tm, LANES), lambda i: (i % pos_tiles, 0))
    out = jax.ShapeDtypeStruct((n, D_MODEL), BF16)
    return pl.pallas_call(
        functools.partial(_inproj_kernel, q_scale=q_scale),
        grid=(n // tm,),
        in_specs=[row_spec, _resident((1, D_MODEL)),
                  _resident((D_MODEL, N_IN_GROUPS * D_MODEL)), tab_spec, tab_spec],
        out_specs=[row_spec] * N_IN_GROUPS,
        out_shape=[out] * N_IN_GROUPS,
        compiler_params=pltpu.CompilerParams(
            dimension_semantics=("arbitrary",), vmem_limit_bytes=VMEM_LIMIT),
        name="inproj",
    )(x2d, g, w_in, cos, sin)


def _rglru_kernel(xr_ref, gr_ref, h0_ref, hist_ref, cw_ref, cb_ref, wa_ref, ba_ref,
                  wx_ref, bx_ref, lam_ref, y_ref, hlast_ref, xtail_ref,
                  ext_ref, a_ref, u_ref, h_ref, *, tc):
    t = pl.program_id(1)

    @pl.when(t == 0)
    def _():
        ext_ref[0:SUBLANES, :] = hist_ref[...]
        h_ref[...] = h0_ref[...]

    x = xr_ref[...].astype(F32)
    ext_ref[SUBLANES:SUBLANES + tc, :] = x
    xc = cb_ref[...] + cw_ref[CONV_W - 1:CONV_W, :] * x
    for j in range(1, CONV_W):
        xc = xc + (cw_ref[CONV_W - 1 - j:CONV_W - j, :]
                   * ext_ref[SUBLANES - j:SUBLANES - j + tc, :])
    ext_ref[0:SUBLANES, :] = x[tc - SUBLANES:tc, :]

    xcb = xc.astype(BF16)

    def gate(w_ref, b_ref):
        parts = [jnp.dot(xcb[:, n * RNN_BLOCK:(n + 1) * RNN_BLOCK], w_ref[n],
                         preferred_element_type=F32) for n in range(N_RNN_BLOCKS)]
        return _sigmoid(jnp.concatenate(parts, axis=1) + b_ref[...])

    r = gate(wa_ref, ba_ref)
    ig = gate(wx_ref, bx_ref)
    z = -lam_ref[...]
    softplus = jnp.maximum(z, 0.0) + jnp.log1p(jnp.exp(-jnp.abs(z)))
    log_a = (-LRU_C) * r * softplus
    a = jnp.exp(log_a)
    th = jnp.tanh(log_a)
    u = jnp.sqrt(-2.0 * th / (1.0 - th)) * (ig * xc)

    nt = tc // SUBLANES
    a3 = a.reshape(nt, SUBLANES, D_MODEL)
    u3 = u.reshape(nt, SUBLANES, D_MODEL)
    row = lax.broadcasted_iota(jnp.int32, (1, SUBLANES, D_MODEL), 1)
    k = 1
    while k < SUBLANES:
        keep = row >= k
        u3 = u3 + a3 * jnp.where(keep, pltpu.roll(u3, k, 1), 0.0)
        a3 = a3 * jnp.where(keep, pltpu.roll(a3, k, 1), 1.0)
        k *= 2
    a_ref[...] = a3.reshape(tc, D_MODEL)
    u_ref[...] = u3.reshape(tc, D_MODEL)

    def tile_step(i, carry):
        off = pl.multiple_of(i * SUBLANES, SUBLANES)
        h = u_ref[pl.ds(off, SUBLANES), :] + a_ref[pl.ds(off, SUBLANES), :] * carry
        u_ref[pl.ds(off, SUBLANES), :] = h
        return jnp.broadcast_to(h[SUBLANES - 1:SUBLANES, :], (SUBLANES, D_MODEL))

    carry = lax.fori_loop(0, nt, tile_step, h_ref[...])
    h_ref[...] = carry

    y_ref[...] = (_gelu_tanh(gr_ref[...].astype(F32)) * u_ref[...]).astype(BF16)
    hlast_ref[0] = carry
    xtail_ref[0] = x[tc - SUBLANES:tc, :]


def _rglru(xr, gr, h0, hist, cw, cb, wa, ba, wx, bx, lam, *, nb, tc):
    n = xr.shape[0]
    nchunk = n // (nb * tc)
    row_spec = pl.BlockSpec((tc, D_MODEL), lambda b, t: (b * nchunk + t, 0))
    state_spec = pl.BlockSpec((1, SUBLANES, D_MODEL), lambda b, t: (b, 0, 0))
    blk = (N_RNN_BLOCKS, RNN_BLOCK, RNN_BLOCK)
    return pl.pallas_call(
        functools.partial(_rglru_kernel, tc=tc),
        grid=(nb, nchunk),
        in_specs=[row_spec, row_spec, _resident((SUBLANES, D_MODEL)),
                  _resident((SUBLANES, D_MODEL)), _resident((CONV_W, D_MODEL)),
                  _resident((1, D_MODEL)), _resident(blk), _resident((1, D_MODEL)),
                  _resident(blk), _resident((1, D_MODEL)), _resident((1, D_MODEL))],
        out_specs=[row_spec, state_spec, state_spec],
        out_shape=[jax.ShapeDtypeStruct((n, D_MODEL), BF16),
                   jax.ShapeDtypeStruct((nb, SUBLANES, D_MODEL), F32),
                   jax.ShapeDtypeStruct((nb, SUBLANES, D_MODEL), F32)],
        scratch_shapes=[pltpu.VMEM((tc + SUBLANES, D_MODEL), F32),
                        pltpu.VMEM((tc, D_MODEL), F32),
                        pltpu.VMEM((tc, D_MODEL), F32),
                        pltpu.VMEM((SUBLANES, D_MODEL), F32)],
        compiler_params=pltpu.CompilerParams(
            dimension_semantics=("arbitrary", "arbitrary"), vmem_limit_bytes=VMEM_LIMIT),
        name="rglru",
    )(xr, gr, h0, hist, cw, cb, wa, ba, wx, bx, lam)


def _attn_kernel(lamp_ref, q_ref, k_ref, v_ref, km_ref, vm_ref, g_ref, o_ref, *, tq):
    i = pl.program_id(2)
    q = q_ref[...]
    lane = lax.broadcasted_iota(jnp.int32, (1, LANES), 1)
    zero = jnp.zeros_like(q)
    qq = jnp.concatenate([jnp.where(lane < HEAD_DIM, q, zero),
                          jnp.where(lane >= HEAD_DIM, q, zero)], axis=0)

    def scores(kt):
        return lax.dot_general(qq, kt, (((1,), (1,)), ((), ())),
                               preferred_element_type=F32)

    s = jnp.where(lane < N_META, scores(km_ref[...]), MASK_VALUE)
    m = jnp.max(s, axis=-1, keepdims=True)
    p = jnp.exp2(s - m)
    l = jnp.sum(p, axis=-1, keepdims=True)
    acc = jnp.dot(p.astype(BF16), vm_ref[...], preferred_element_type=F32)

    def update(carry, s, vt):
        m, l, acc = carry
        m_new = jnp.maximum(m, jnp.max(s, axis=-1, keepdims=True))
        alpha = jnp.exp2(m - m_new)
        p = jnp.exp2(s - m_new)
        l = alpha * l + jnp.sum(p, axis=-1, keepdims=True)
        acc = alpha * acc + jnp.dot(p.astype(BF16), vt, preferred_element_type=F32)
        return m_new, l, acc

    def full_tile(j, carry):
        off = pl.multiple_of(j * tq, tq)
        return update(carry, scores(k_ref[pl.ds(off, tq), :]), v_ref[pl.ds(off, tq), :])

    carry = lax.fori_loop(0, i, full_tile, (m, l, acc))

    off = pl.multiple_of(i * tq, tq)
    s = scores(k_ref[pl.ds(off, tq), :])
    qpos = lax.broadcasted_iota(jnp.int32, (2 * tq, tq), 0) & (tq - 1)
    kpos = lax.broadcasted_iota(jnp.int32, (2 * tq, tq), 1)
    s = jnp.where(kpos <= qpos, s, MASK_VALUE)
    m, l, acc = update(carry, s, v_ref[pl.ds(off, tq), :])

    lp = lamp_ref[...]
    lam = (jnp.exp(jnp.sum(lp[0:1] * lp[1:2], axis=-1, keepdims=True))
           - jnp.exp(jnp.sum(lp[2:3] * lp[3:4], axis=-1, keepdims=True)) + LAM_INIT)
    o = acc[:tq] / l[:tq] - lam * (acc[tq:] / l[tq:])
    o = o * _rms_scale(o) * g_ref[...] * (1.0 - LAM_INIT)
    o_ref[...] = o.astype(BF16)


def _attention(lamp, q, k, v, km, vm, g_subln, *, nb, seq, tq):
    nq = seq // tq
    q_spec = pl.BlockSpec((tq, V_DIM), lambda b, h, i: (b * nq + i, h))
    kv_spec = pl.BlockSpec((seq, V_DIM), lambda b, h, i: (b, h))
    meta_spec = pl.BlockSpec((LANES, V_DIM), lambda b, h, i: (0, h))
    return pl.pallas_call(
        functools.partial(_attn_kernel, tq=tq),
        grid=(nb, N_HEADS, nq),
        in_specs=[pl.BlockSpec((4, HEAD_DIM), lambda b, h, i: (0, 0)),
                  q_spec, kv_spec, kv_spec, meta_spec, meta_spec,
                  pl.BlockSpec((1, V_DIM), lambda b, h, i: (0, 0))],
        out_specs=q_spec,
        out_shape=jax.ShapeDtypeStruct(q.shape, BF16),
        compiler_params=pltpu.CompilerParams(
            dimension_semantics=("arbitrary", "arbitrary", "arbitrary"),
            vmem_limit_bytes=VMEM_LIMIT),
        name="diffattn",
    )(lamp, q, k, v, km, vm, g_subln)


def _merge_kernel(x_ref, y_ref, o_ref, sgr_ref, sga_ref, wr_ref, wa_ref, wo_ref, h_ref):
    y_r = jnp.dot(y_ref[...], wr_ref[...], preferred_element_type=F32)
    y_a = jnp.dot(o_ref[...], wa_ref[...], preferred_element_type=F32)
    merged = sgr_ref[...].astype(F32) * y_r + sga_ref[...].astype(F32) * y_a
    h_ref[...] = x_ref[...] + jnp.dot(merged.astype(BF16), wo_ref[...],
                                      preferred_element_type=F32)


def _merge(x2d, y, o, sgr, sga, wr, wa, wo, *, tm):
    n = x2d.shape[0]
    row_spec = pl.BlockSpec((tm, D_MODEL), lambda i: (i, 0))
    w_spec = _resident((D_MODEL, D_MODEL))
    return pl.pallas_call(
        _merge_kernel,
        grid=(n // tm,),
        in_specs=[row_spec] * 5 + [w_spec] * 3,
        out_specs=row_spec,
        out_shape=jax.ShapeDtypeStruct((n, D_MODEL), F32),
        compiler_params=pltpu.CompilerParams(
            dimension_semantics=("arbitrary",), vmem_limit_bytes=VMEM_LIMIT),
        name="merge",
    )(x2d, y, o, sgr, sga, wr, wa, wo)


def _mlp_kernel(h_ref, g_ref, w1_ref, w2_ref, gf_ref, out_ref):
    h = h_ref[...]
    hn = (h * _rms_scale(h) * g_ref[...]).astype(BF16)
    acc = h
    for c in range(D_FF // D_MODEL):
        ff = jnp.dot(hn, w1_ref[:, c * D_MODEL:(c + 1) * D_MODEL],
                     preferred_element_type=F32)
        ff = jnp.square(jnp.maximum(ff, 0.0)).astype(BF16)
        acc = acc + jnp.dot(ff, w2_ref[c * D_MODEL:(c + 1) * D_MODEL, :],
                            preferred_element_type=F32)
    out_ref[...] = acc * _rms_scale(acc) * gf_ref[...]


def _mlp(h, g, w1, w2, gf, *, tm):
    n = h.shape[0]
    row_spec = pl.BlockSpec((tm, D_MODEL), lambda i: (i, 0))
    return pl.pallas_call(
        _mlp_kernel,
        grid=(n // tm,),
        in_specs=[row_spec, _resident((1, D_MODEL)), _resident((D_MODEL, D_FF)),
                  _resident((D_FF, D_MODEL)), _resident((1, D_MODEL))],
        out_specs=row_spec,
        out_shape=jax.ShapeDtypeStruct((n, D_MODEL), F32),
        compiler_params=pltpu.CompilerParams(
            dimension_semantics=("arbitrary",), vmem_limit_bytes=VMEM_LIMIT),
        name="mlp",
    )(h, g, w1, w2, gf)


def _rope_tables(n_pos):
    inv = 1.0 / (ROPE_THETA ** (jnp.arange(0, HEAD_DIM, 2, dtype=F32) / HEAD_DIM))
    ang = jnp.arange(n_pos, dtype=F32)[:, None] * inv[None, :]
    cos, sin = jnp.cos(ang), jnp.sin(ang)
    reps = LANES // HEAD_DIM
    return (jnp.tile(cos, (1, 2 * reps)),
            jnp.tile(jnp.concatenate([-sin, sin], axis=1), (1, reps)))


def kernel(x, meta_tokens, g_mix, w_in, conv_w, conv_b, w_a, b_a, w_x, b_x, lru_lambda,
           lam_q1, lam_k1, lam_q2, lam_k2, g_subln, w_rnn_out, w_attn_out, w_o,
           g_mlp, w_ff1, w_ff2, g_final):
    nb, seq, _ = x.shape
    tm = 512
    tc = 512
    tq = 512
    x2d = x.reshape(nb * seq, D_MODEL)
    row = lambda p: p.reshape(1, -1).astype(F32)

    cos, sin = _rope_tables(N_META + seq)
    q_scale = math.log2(math.e) / math.sqrt(HEAD_DIM)
    w_in_b = w_in[0].astype(BF16)
    g_mix_r = row(g_mix[0])

    inproj = functools.partial(_inproj, q_scale=q_scale)
    xr_m, gr_m, _, k_m, v_m, _, _ = inproj(
        meta_tokens.astype(F32), g_mix_r, w_in_b, cos[:N_META], sin[:N_META],
        tm=N_META, pos_tiles=1)
    xr, gr, q, k, v, sgr, sga = inproj(
        x2d, g_mix_r, w_in_b, cos[N_META:], sin[N_META:], tm=tm, pos_tiles=seq // tm)

    rnn_params = (conv_w[0].astype(F32), row(conv_b[0]), w_a[0].astype(BF16),
                  row(b_a[0]), w_x[0].astype(BF16), row(b_x[0]), row(lru_lambda[0]))
    zeros = jnp.zeros((SUBLANES, D_MODEL), F32)
    _, h_meta, xtail_meta = _rglru(xr_m, gr_m, zeros, zeros, *rnn_params, nb=1, tc=N_META)
    y, _, _ = _rglru(xr, gr, h_meta[0], xtail_meta[0], *rnn_params, nb=nb, tc=tc)

    lamp = jnp.stack([lam_q1[0], lam_k1[0], lam_q2[0], lam_k2[0]]).astype(F32)
    pad = ((0, LANES - N_META), (0, 0))
    o = _attention(lamp, q, k, v, jnp.pad(k_m, pad), jnp.pad(v_m, pad), row(g_subln[0]),
                   nb=nb, seq=seq, tq=tq)

    h1 = _merge(x2d, y, o, sgr, sga, w_rnn_out[0].astype(BF16),
                w_attn_out[0].astype(BF16), w_o[0].astype(BF16), tm=tm)
    out = _mlp(h1, row(g_mlp[0]), w_ff1[0].astype(BF16), w_ff2[0].astype(BF16),
               row(g_final), tm=tm)
    return out.reshape(nb, seq, D_MODEL)
```

```python
import functools
import math

import jax
import jax.numpy as jnp
from jax import lax
from jax.experimental import pallas as pl
from jax.experimental.pallas import tpu as pltpu

D_MODEL = 1024
N_META = 16
N_RNN_BLOCKS = 8
RNN_BLOCK = D_MODEL // N_RNN_BLOCKS
CONV_W = 4
LRU_C = 8.0
N_HEADS = 8
HEAD_DIM = 64
V_DIM = 128
ROPE_THETA = 10000.0
D_FF = 4 * D_MODEL
EPS = 1e-6
N_IN_GROUPS = 7
LAM_INIT = 0.8 - 0.6 * math.exp(-0.3 * 0)

LANES = 128
SUBLANES = 8
VMEM_LIMIT = 56 * 1024 * 1024
MASK_VALUE = -1e30
SAFE_SUM = 2.0 ** 100
SAFE_OUT = 3.0e38

BF16 = jnp.bfloat16
F32 = jnp.float32


def _sigmoid(x):
    return 0.5 * jnp.tanh(0.5 * x) + 0.5


def _gelu_tanh(x):
    c = math.sqrt(2.0 / math.pi)
    return x * (0.5 * (1.0 + jnp.tanh(c * (x + 0.044715 * (x * x * x)))))


def _rms_scale(xf):
    return lax.rsqrt(jnp.mean(xf * xf, axis=-1, keepdims=True) + EPS)


def _resident(shape):
    nd = len(shape)
    return pl.BlockSpec(shape, lambda *_: (0,) * nd, pipeline_mode=pl.Buffered(1))


def _inproj_kernel(x_ref, g_ref, w_ref, cos_ref, sin_ref,
                   xr_ref, gr_ref, q_ref, k_ref, v_ref, sgr_ref, sga_ref, *, q_scale):
    xf = x_ref[...]
    hn = (xf * _rms_scale(xf) * g_ref[...]).astype(BF16)
    cos = cos_ref[...]
    sin = sin_ref[...]
    lane = lax.broadcasted_iota(jnp.int32, (1, LANES), 1)
    first_half = (lane & (HEAD_DIM // 2)) == 0

    def proj(c):
        return jnp.dot(hn, w_ref[:, c * D_MODEL:(c + 1) * D_MODEL],
                       preferred_element_type=F32)

    def rope_store(r, out_ref, scale):
        for j in range(D_MODEL // LANES):
            xg = r[:, j * LANES:(j + 1) * LANES]
            rot = jnp.where(first_half,
                            pltpu.roll(xg, LANES - HEAD_DIM // 2, 1),
                            pltpu.roll(xg, HEAD_DIM // 2, 1))
            o = xg * cos + rot * sin
            if scale != 1.0:
                o = o * scale
            out_ref[:, j * LANES:(j + 1) * LANES] = o.astype(BF16)

    xr_ref[...] = proj(0).astype(BF16)
    gr_ref[...] = proj(1).astype(BF16)
    rope_store(proj(2), q_ref, q_scale)
    rope_store(proj(3), k_ref, 1.0)
    v_ref[...] = proj(4).astype(BF16)
    sgr_ref[...] = _sigmoid(proj(5)).astype(BF16)
    sga_ref[...] = _sigmoid(proj(6)).astype(BF16)


def _inproj(x2d, g, w_in, cos, sin, *, tm, pos_tiles, q_scale):
    n = x2d.shape[0]
    row_spec = pl.BlockSpec((tm, D_MODEL), lambda i: (i, 0))
    tab_spec = pl.BlockSpec((tm, LANES), lambda i: (i % pos_tiles, 0))
    out = jax.ShapeDtypeStruct((n, D_MODEL), BF16)
    return pl.pallas_call(
        functools.partial(_inproj_kernel, q_scale=q_scale),
        grid=(n // tm,),
        in_specs=[row_spec, _resident((1, D_MODEL)),
                  _resident((D_MODEL, N_IN_GROUPS * D_MODEL)), tab_spec, tab_spec],
        out_specs=[row_spec] * N_IN_GROUPS,
        out_shape=[out] * N_IN_GROUPS,
        compiler_params=pltpu.CompilerParams(
            dimension_semantics=("arbitrary",), vmem_limit_bytes=VMEM_LIMIT),
        name="inproj",
    )(x2d, g, w_in, cos, sin)


def _rglru_kernel(xr_ref, gr_ref, h0_ref, hist_ref, cw_ref, cb_ref, wa_ref, ba_ref,
                  wx_ref, bx_ref, lam_ref, y_ref, hlast_ref, xtail_ref,
                  ext_ref, a_ref, u_ref, h_ref, *, tc):
    t = pl.program_id(1)

    @pl.when(t == 0)
    def _():
        ext_ref[0:SUBLANES, :] = hist_ref[...]
        h_ref[...] = h0_ref[...]

    x = xr_ref[...].astype(F32)
    ext_ref[SUBLANES:SUBLANES + tc, :] = x
    xc = cb_ref[...] + cw_ref[CONV_W - 1:CONV_W, :] * x
    for j in range(1, CONV_W):
        xc = xc + (cw_ref[CONV_W - 1 - j:CONV_W - j, :]
                   * ext_ref[SUBLANES - j:SUBLANES - j + tc, :])
    ext_ref[0:SUBLANES, :] = x[tc - SUBLANES:tc, :]

    xcb = xc.astype(BF16)

    def gate(w_ref, b_ref):
        parts = [jnp.dot(xcb[:, n * RNN_BLOCK:(n + 1) * RNN_BLOCK], w_ref[n],
                         preferred_element_type=F32) for n in range(N_RNN_BLOCKS)]
        return _sigmoid(jnp.concatenate(parts, axis=1) + b_ref[...])

    r = gate(wa_ref, ba_ref)
    ig = gate(wx_ref, bx_ref)
    z = -lam_ref[...]
    softplus = jnp.maximum(z, 0.0) + jnp.log1p(jnp.exp(-jnp.abs(z)))
    log_a = (-LRU_C) * r * softplus
    a = jnp.exp(log_a)
    th = jnp.tanh(log_a)
    u = jnp.sqrt(-2.0 * th / (1.0 - th)) * (ig * xc)

    nt = tc // SUBLANES
    a3 = a.reshape(nt, SUBLANES, D_MODEL)
    u3 = u.reshape(nt, SUBLANES, D_MODEL)
    row = lax.broadcasted_iota(jnp.int32, (1, SUBLANES, D_MODEL), 1)
    k = 1
    while k < SUBLANES:
        keep = row >= k
        u3 = u3 + a3 * jnp.where(keep, pltpu.roll(u3, k, 1), 0.0)
        a3 = a3 * jnp.where(keep, pltpu.roll(a3, k, 1), 1.0)
        k *= 2
    a_ref[...] = a3.reshape(tc, D_MODEL)
    u_ref[...] = u3.reshape(tc, D_MODEL)

    def tile_step(i, carry):
        off = pl.multiple_of(i * SUBLANES, SUBLANES)
        h = u_ref[pl.ds(off, SUBLANES), :] + a_ref[pl.ds(off, SUBLANES), :] * carry
        u_ref[pl.ds(off, SUBLANES), :] = h
        return jnp.broadcast_to(h[SUBLANES - 1:SUBLANES, :], (SUBLANES, D_MODEL))

    carry = lax.fori_loop(0, nt, tile_step, h_ref[...])
    h_ref[...] = carry

    y_ref[...] = (_gelu_tanh(gr_ref[...].astype(F32)) * u_ref[...]).astype(BF16)
    hlast_ref[0] = carry
    xtail_ref[0] = x[tc - SUBLANES:tc, :]


def _rglru(xr, gr, h0, hist, cw, cb, wa, ba, wx, bx, lam, *, nb, tc):
    n = xr.shape[0]
    nchunk = n // (nb * tc)
    row_spec = pl.BlockSpec((tc, D_MODEL), lambda b, t: (b * nchunk + t, 0))
    state_spec = pl.BlockSpec((1, SUBLANES, D_MODEL), lambda b, t: (b, 0, 0))
    blk = (N_RNN_BLOCKS, RNN_BLOCK, RNN_BLOCK)
    return pl.pallas_call(
        functools.partial(_rglru_kernel, tc=tc),
        grid=(nb, nchunk),
        in_specs=[row_spec, row_spec, _resident((SUBLANES, D_MODEL)),
                  _resident((SUBLANES, D_MODEL)), _resident((CONV_W, D_MODEL)),
                  _resident((1, D_MODEL)), _resident(blk), _resident((1, D_MODEL)),
                  _resident(blk), _resident((1, D_MODEL)), _resident((1, D_MODEL))],
        out_specs=[row_spec, state_spec, state_spec],
        out_shape=[jax.ShapeDtypeStruct((n, D_MODEL), BF16),
                   jax.ShapeDtypeStruct((nb, SUBLANES, D_MODEL), F32),
                   jax.ShapeDtypeStruct((nb, SUBLANES, D_MODEL), F32)],
        scratch_shapes=[pltpu.VMEM((tc + SUBLANES, D_MODEL), F32),
                        pltpu.VMEM((tc, D_MODEL), F32),
                        pltpu.VMEM((tc, D_MODEL), F32),
                        pltpu.VMEM((SUBLANES, D_MODEL), F32)],
        compiler_params=pltpu.CompilerParams(
            dimension_semantics=("arbitrary", "arbitrary"), vmem_limit_bytes=VMEM_LIMIT),
        name="rglru",
    )(xr, gr, h0, hist, cw, cb, wa, ba, wx, bx, lam)


def _attn_kernel(lamp_ref, q_ref, k_ref, v_ref, km_ref, vm_ref, g_ref, o_ref,
                 m_ref, l_ref, acc_ref, *, tq):
    i = pl.program_id(2)
    q = q_ref[...]
    lane = lax.broadcasted_iota(jnp.int32, (1, LANES), 1)
    zero = jnp.zeros_like(q)
    qq = jnp.concatenate([jnp.where(lane < HEAD_DIM, q, zero),
                          jnp.where(lane >= HEAD_DIM, q, zero)], axis=0)
    diag_off = pl.multiple_of(i * tq, tq)

    def scores(kt):
        return lax.dot_general(qq, kt, (((1,), (1,)), ((), ())),
                               preferred_element_type=F32)

    def causal(s):
        qpos = lax.broadcasted_iota(jnp.int32, s.shape, 0) & (tq - 1)
        kpos = lax.broadcasted_iota(jnp.int32, s.shape, 1)
        return jnp.where(kpos <= qpos, s, MASK_VALUE)

    def finish(l, acc):
        lp = lamp_ref[...]
        lam = (jnp.exp(jnp.sum(lp[0:1] * lp[1:2], axis=-1, keepdims=True))
               - jnp.exp(jnp.sum(lp[2:3] * lp[3:4], axis=-1, keepdims=True)) + LAM_INIT)
        o = acc[:tq] / l[:tq] - lam * (acc[tq:] / l[tq:])
        return o * _rms_scale(o) * g_ref[...] * (1.0 - LAM_INIT)

    s_meta = jnp.where(lane < N_META, scores(km_ref[...]), MASK_VALUE)
    m_meta = jnp.max(s_meta, axis=-1, keepdims=True)

    m_ref[...] = jnp.broadcast_to(m_meta, m_ref.shape)

    def accumulate(s, vt, first=False):
        m = m_ref[...]
        parts = [jnp.exp2(s[:, c * LANES:(c + 1) * LANES] - m)
                 for c in range(s.shape[1] // LANES)]
        lsum = parts[0]
        for pc in parts[1:]:
            lsum = lsum + pc
        p = jnp.concatenate(parts, axis=1) if len(parts) > 1 else parts[0]
        pv = jnp.dot(p.astype(BF16), vt, preferred_element_type=F32)
        if first:
            l_ref[...] = lsum
            acc_ref[...] = pv
        else:
            l_ref[...] += lsum
            acc_ref[...] += pv

    accumulate(s_meta, vm_ref[...], first=True)

    def fast_tile(j, _):
        off = pl.multiple_of(j * tq, tq)
        accumulate(scores(k_ref[pl.ds(off, tq), :]), v_ref[pl.ds(off, tq), :])
        return 0

    lax.fori_loop(0, i, fast_tile, 0)
    accumulate(causal(scores(k_ref[pl.ds(diag_off, tq), :])), v_ref[pl.ds(diag_off, tq), :])

    l = jnp.sum(l_ref[...], axis=-1, keepdims=True)
    o = finish(l, acc_ref[...])
    o_ref[...] = o.astype(BF16)

    l_ok = jnp.logical_and(l[:tq] < SAFE_SUM, l[tq:] < SAFE_SUM)
    o_ok = jnp.max(jnp.abs(o), axis=-1, keepdims=True) < SAFE_OUT
    all_ok = jnp.min(jnp.where(jnp.logical_and(l_ok, o_ok), 1.0, 0.0))

    @pl.when(all_ok < 0.5)
    def _():
        def update(carry, s, vt):
            m, l, acc = carry
            m_new = jnp.maximum(m, jnp.max(s, axis=-1, keepdims=True))
            alpha = jnp.exp2(m - m_new)
            p = jnp.exp2(s - m_new)
            l = alpha * l + jnp.sum(p, axis=-1, keepdims=True)
            acc = alpha * acc + jnp.dot(p.astype(BF16), vt, preferred_element_type=F32)
            return m_new, l, acc

        def safe_tile(j, carry):
            off = pl.multiple_of(j * tq, tq)
            return update(carry, scores(k_ref[pl.ds(off, tq), :]),
                          v_ref[pl.ds(off, tq), :])

        p = jnp.exp2(s_meta - m_meta)
        carry = (m_meta, jnp.sum(p, axis=-1, keepdims=True),
                 jnp.dot(p.astype(BF16), vm_ref[...], preferred_element_type=F32))
        carry = lax.fori_loop(0, i, safe_tile, carry)
        _, l, acc = update(carry, causal(scores(k_ref[pl.ds(diag_off, tq), :])),
                           v_ref[pl.ds(diag_off, tq), :])
        o_ref[...] = finish(l, acc).astype(BF16)


def _attention(lamp, q, k, v, km, vm, g_subln, *, nb, seq, tq):
    nq = seq // tq
    q_spec = pl.BlockSpec((tq, V_DIM), lambda b, h, i: (b * nq + i, h))
    kv_spec = pl.BlockSpec((seq, V_DIM), lambda b, h, i: (b, h))
    meta_spec = pl.BlockSpec((LANES, V_DIM), lambda b, h, i: (0, h))
    return pl.pallas_call(
        functools.partial(_attn_kernel, tq=tq),
        grid=(nb, N_HEADS, nq),
        in_specs=[pl.BlockSpec((4, HEAD_DIM), lambda b, h, i: (0, 0)),
                  q_spec, kv_spec, kv_spec, meta_spec, meta_spec,
                  pl.BlockSpec((1, V_DIM), lambda b, h, i: (0, 0))],
        out_specs=q_spec,
        out_shape=jax.ShapeDtypeStruct(q.shape, BF16),
        scratch_shapes=[pltpu.VMEM((2 * tq, LANES), F32)] * 3,
        compiler_params=pltpu.CompilerParams(
            dimension_semantics=("arbitrary", "arbitrary", "arbitrary"),
            vmem_limit_bytes=VMEM_LIMIT),
        name="diffattn",
    )(lamp, q, k, v, km, vm, g_subln)


def _merge_kernel(x_ref, y_ref, o_ref, sgr_ref, sga_ref, wr_ref, wa_ref, wo_ref, h_ref):
    y_r = jnp.dot(y_ref[...], wr_ref[...], preferred_element_type=F32)
    y_a = jnp.dot(o_ref[...], wa_ref[...], preferred_element_type=F32)
    merged = sgr_ref[...].astype(F32) * y_r + sga_ref[...].astype(F32) * y_a
    h_ref[...] = x_ref[...] + jnp.dot(merged.astype(BF16), wo_ref[...],
                                      preferred_element_type=F32)


def _merge(x2d, y, o, sgr, sga, wr, wa, wo, *, tm):
    n = x2d.shape[0]
    row_spec = pl.BlockSpec((tm, D_MODEL), lambda i: (i, 0))
    w_spec = _resident((D_MODEL, D_MODEL))
    return pl.pallas_call(
        _merge_kernel,
        grid=(n // tm,),
        in_specs=[row_spec] * 5 + [w_spec] * 3,
        out_specs=row_spec,
        out_shape=jax.ShapeDtypeStruct((n, D_MODEL), F32),
        compiler_params=pltpu.CompilerParams(
            dimension_semantics=("arbitrary",), vmem_limit_bytes=VMEM_LIMIT),
        name="merge",
    )(x2d, y, o, sgr, sga, wr, wa, wo)


def _mlp_kernel(h_ref, g_ref, w1_ref, w2_ref, gf_ref, out_ref):
    h = h_ref[...]
    hn = (h * _rms_scale(h) * g_ref[...]).astype(BF16)
    acc = h
    for c in range(D_FF // D_MODEL):
        ff = jnp.dot(hn, w1_ref[:, c * D_MODEL:(c + 1) * D_MODEL],
                     preferred_element_type=F32)
        ff = jnp.square(jnp.maximum(ff, 0.0)).astype(BF16)
        acc = acc + jnp.dot(ff, w2_ref[c * D_MODEL:(c + 1) * D_MODEL, :],
                            preferred_element_type=F32)
    out_ref[...] = acc * _rms_scale(acc) * gf_ref[...]


def _mlp(h, g, w1, w2, gf, *, tm):
    n = h.shape[0]
    row_spec = pl.BlockSpec((tm, D_MODEL), lambda i: (i, 0))
    return pl.pallas_call(
        _mlp_kernel,
        grid=(n // tm,),
        in_specs=[row_spec, _resident((1, D_MODEL)), _resident((D_MODEL, D_FF)),
                  _resident((D_FF, D_MODEL)), _resident((1, D_MODEL))],
        out_specs=row_spec,
        out_shape=jax.ShapeDtypeStruct((n, D_MODEL), F32),
        compiler_params=pltpu.CompilerParams(
            dimension_semantics=("arbitrary",), vmem_limit_bytes=VMEM_LIMIT),
        name="mlp",
    )(h, g, w1, w2, gf)


def _rope_tables(n_pos):
    inv = 1.0 / (ROPE_THETA ** (jnp.arange(0, HEAD_DIM, 2, dtype=F32) / HEAD_DIM))
    ang = jnp.arange(n_pos, dtype=F32)[:, None] * inv[None, :]
    cos, sin = jnp.cos(ang), jnp.sin(ang)
    reps = LANES // HEAD_DIM
    return (jnp.tile(cos, (1, 2 * reps)),
            jnp.tile(jnp.concatenate([-sin, sin], axis=1), (1, reps)))


def kernel(x, meta_tokens, g_mix, w_in, conv_w, conv_b, w_a, b_a, w_x, b_x, lru_lambda,
           lam_q1, lam_k1, lam_q2, lam_k2, g_subln, w_rnn_out, w_attn_out, w_o,
           g_mlp, w_ff1, w_ff2, g_final):
    nb, seq, _ = x.shape
    tm = 512
    tc = 512
    tq = 512
    x2d = x.reshape(nb * seq, D_MODEL)
    row = lambda p: p.reshape(1, -1).astype(F32)

    cos, sin = _rope_tables(N_META + seq)
    q_scale = math.log2(math.e) / math.sqrt(HEAD_DIM)
    w_in_b = w_in[0].astype(BF16)
    g_mix_r = row(g_mix[0])

    inproj = functools.partial(_inproj, q_scale=q_scale)
    xr_m, gr_m, _, k_m, v_m, _, _ = inproj(
        meta_tokens.astype(F32), g_mix_r, w_in_b, cos[:N_META], sin[:N_META],
        tm=N_META, pos_tiles=1)
    xr, gr, q, k, v, sgr, sga = inproj(
        x2d, g_mix_r, w_in_b, cos[N_META:], sin[N_META:], tm=tm, pos_tiles=seq // tm)

    rnn_params = (conv_w[0].astype(F32), row(conv_b[0]), w_a[0].astype(BF16),
                  row(b_a[0]), w_x[0].astype(BF16), row(b_x[0]), row(lru_lambda[0]))
    zeros = jnp.zeros((SUBLANES, D_MODEL), F32)
    _, h_meta, xtail_meta = _rglru(xr_m, gr_m, zeros, zeros, *rnn_params, nb=1, tc=N_META)
    y, _, _ = _rglru(xr, gr, h_meta[0], xtail_meta[0], *rnn_params, nb=nb, tc=tc)

    lamp = jnp.stack([lam_q1[0], lam_k1[0], lam_q2[0], lam_k2[0]]).astype(F32)
    pad = ((0, LANES - N_META), (0, 0))
    o = _attention(lamp, q, k, v, jnp.pad(k_m, pad), jnp.pad(v_m, pad), row(g_subln[0]),
                   nb=nb, seq=seq, tq=tq)

    h1 = _merge(x2d, y, o, sgr, sga, w_rnn_out[0].astype(BF16),
                w_attn_out[0].astype(BF16), w_o[0].astype(BF16), tm=tm)
    out = _mlp(h1, row(g_mlp[0]), w_ff1[0].astype(BF16), w_ff2[0].astype(BF16),
               row(g_final), tm=tm)
    return out.reshape(nb, seq, D_MODEL)
```

```python
import functools
import math

import jax
import jax.numpy as jnp
from jax import lax
from jax.experimental import pallas as pl
from jax.experimental.pallas import tpu as pltpu

D_MODEL = 1024
N_META = 16
N_RNN_BLOCKS = 8
RNN_BLOCK = D_MODEL // N_RNN_BLOCKS
CONV_W = 4
LRU_C = 8.0
N_HEADS = 8
HEAD_DIM = 64
V_DIM = 128
ROPE_THETA = 10000.0
D_FF = 4 * D_MODEL
EPS = 1e-6
N_IN_GROUPS = 7
LAM_INIT = 0.8 - 0.6 * math.exp(-0.3 * 0)

LANES = 128
SUBLANES = 8
VMEM_LIMIT = 56 * 1024 * 1024
MASK_VALUE = -1e30
SAFE_SUM = 2.0 ** 100
SAFE_OUT = 3.0e38
KV_RUN = 2048

BF16 = jnp.bfloat16
F32 = jnp.float32


def _sigmoid(x):
    return 0.5 * jnp.tanh(0.5 * x) + 0.5


def _gelu_tanh(x):
    c = math.sqrt(2.0 / math.pi)
    return (0.5 * x) * (1.0 + jnp.tanh(x * ((c * 0.044715) * (x * x) + c)))


def _rms_scale(xf):
    return lax.rsqrt(jnp.mean(xf * xf, axis=-1, keepdims=True) + EPS)


def _resident(shape):
    nd = len(shape)
    return pl.BlockSpec(shape, lambda *_: (0,) * nd, pipeline_mode=pl.Buffered(1))


def _inproj_kernel(x_ref, g_ref, w_ref, cos_ref, sin_ref,
                   xr_ref, gr_ref, q_ref, k_ref, v_ref, sgr_ref, sga_ref, *, q_scale):
    xf = x_ref[...]
    hn = (xf * _rms_scale(xf) * g_ref[...]).astype(BF16)
    cos = cos_ref[...]
    sin = sin_ref[...]
    lane = lax.broadcasted_iota(jnp.int32, (1, LANES), 1)
    first_half = (lane & (HEAD_DIM // 2)) == 0

    def proj(c):
        return jnp.dot(hn, w_ref[:, c * D_MODEL:(c + 1) * D_MODEL],
                       preferred_element_type=F32)

    def rope_store(r, out_ref, scale):
        for j in range(D_MODEL // LANES):
            xg = r[:, j * LANES:(j + 1) * LANES]
            rot = jnp.where(first_half,
                            pltpu.roll(xg, LANES - HEAD_DIM // 2, 1),
                            pltpu.roll(xg, HEAD_DIM // 2, 1))
            o = xg * cos + rot * sin
            if scale != 1.0:
                o = o * scale
            out_ref[:, j * LANES:(j + 1) * LANES] = o.astype(BF16)

    xr_ref[...] = proj(0).astype(BF16)
    gr_ref[...] = proj(1).astype(BF16)
    rope_store(proj(2), q_ref, q_scale)
    rope_store(proj(3), k_ref, 1.0)
    v_ref[...] = proj(4).astype(BF16)
    sgr_ref[...] = _sigmoid(proj(5)).astype(BF16)
    sga_ref[...] = _sigmoid(proj(6)).astype(BF16)


def _inproj(x2d, g, w_in, cos, sin, *, tm, pos_tiles, q_scale):
    n = x2d.shape[0]
    row_spec = pl.BlockSpec((tm, D_MODEL), lambda i: (i, 0))
    tab_spec = pl.BlockSpec((tm, LANES), lambda i: (i % pos_tiles, 0))
    out = jax.ShapeDtypeStruct((n, D_MODEL), BF16)
    return pl.pallas_call(
        functools.partial(_inproj_kernel, q_scale=q_scale),
        grid=(n // tm,),
        in_specs=[row_spec, _resident((1, D_MODEL)),
                  _resident((D_MODEL, N_IN_GROUPS * D_MODEL)), tab_spec, tab_spec],
        out_specs=[row_spec] * N_IN_GROUPS,
        out_shape=[out] * N_IN_GROUPS,
        compiler_params=pltpu.CompilerParams(
            dimension_semantics=("arbitrary",), vmem_limit_bytes=VMEM_LIMIT),
        name="inproj",
    )(x2d, g, w_in, cos, sin)


def _rglru_kernel(xr_ref, gr_ref, h0_ref, hist_ref, cw_ref, cb_ref, wa_ref, ba_ref,
                  wx_ref, bx_ref, lam_ref, y_ref, hlast_ref, xtail_ref,
                  ext_ref, a_ref, u_ref, h_ref, *, tc):
    t = pl.program_id(1)

    @pl.when(t == 0)
    def _():
        ext_ref[0:SUBLANES, :] = hist_ref[...]
        h_ref[...] = h0_ref[...]

    x = xr_ref[...].astype(F32)
    ext_ref[SUBLANES:SUBLANES + tc, :] = x
    xc = cb_ref[...] + cw_ref[CONV_W - 1:CONV_W, :] * x
    for j in range(1, CONV_W):
        xc = xc + (cw_ref[CONV_W - 1 - j:CONV_W - j, :]
                   * ext_ref[SUBLANES - j:SUBLANES - j + tc, :])
    ext_ref[0:SUBLANES, :] = x[tc - SUBLANES:tc, :]

    xcb = xc.astype(BF16)

    def gate_tanh(w_ref, b_ref):
        parts = [jnp.dot(xcb[:, n * RNN_BLOCK:(n + 1) * RNN_BLOCK], w_ref[n],
                         preferred_element_type=F32) for n in range(N_RNN_BLOCKS)]
        return jnp.tanh(jnp.concatenate(parts, axis=1) + b_ref[...])

    tr = gate_tanh(wa_ref, ba_ref)
    ti = gate_tanh(wx_ref, bx_ref)
    z = -lam_ref[...]
    softplus = jnp.maximum(z, 0.0) + jnp.log1p(jnp.exp(-jnp.abs(z)))
    half_rate = (-0.5 * LRU_C) * softplus
    log_a = half_rate * tr + half_rate
    a = jnp.exp(log_a)
    th = jnp.tanh(log_a)
    u = jnp.sqrt(-0.5 * th / (1.0 - th)) * ((ti + 1.0) * xc)

    nt = tc // SUBLANES
    a3 = a.reshape(nt, SUBLANES, D_MODEL)
    u3 = u.reshape(nt, SUBLANES, D_MODEL)
    row = lax.broadcasted_iota(jnp.int32, (1, SUBLANES, D_MODEL), 1)
    k = 1
    while k < SUBLANES:
        keep = row >= k
        u3 = u3 + a3 * jnp.where(keep, pltpu.roll(u3, k, 1), 0.0)
        a3 = a3 * jnp.where(keep, pltpu.roll(a3, k, 1), 1.0)
        k *= 2
    a_ref[...] = a3.reshape(tc, D_MODEL)
    u_ref[...] = u3.reshape(tc, D_MODEL)

    def tile_step(i, carry):
        off = pl.multiple_of(i * SUBLANES, SUBLANES)
        h = u_ref[pl.ds(off, SUBLANES), :] + a_ref[pl.ds(off, SUBLANES), :] * carry
        u_ref[pl.ds(off, SUBLANES), :] = h
        return jnp.broadcast_to(h[SUBLANES - 1:SUBLANES, :], (SUBLANES, D_MODEL))

    carry = lax.fori_loop(0, nt, tile_step, h_ref[...])
    h_ref[...] = carry

    y_ref[...] = (_gelu_tanh(gr_ref[...].astype(F32)) * u_ref[...]).astype(BF16)
    hlast_ref[0] = carry
    xtail_ref[0] = x[tc - SUBLANES:tc, :]


def _rglru(xr, gr, h0, hist, cw, cb, wa, ba, wx, bx, lam, *, nb, tc):
    n = xr.shape[0]
    nchunk = n // (nb * tc)
    row_spec = pl.BlockSpec((tc, D_MODEL), lambda b, t: (b * nchunk + t, 0))
    state_spec = pl.BlockSpec((1, SUBLANES, D_MODEL), lambda b, t: (b, 0, 0))
    blk = (N_RNN_BLOCKS, RNN_BLOCK, RNN_BLOCK)
    return pl.pallas_call(
        functools.partial(_rglru_kernel, tc=tc),
        grid=(nb, nchunk),
        in_specs=[row_spec, row_spec, _resident((SUBLANES, D_MODEL)),
                  _resident((SUBLANES, D_MODEL)), _resident((CONV_W, D_MODEL)),
                  _resident((1, D_MODEL)), _resident(blk), _resident((1, D_MODEL)),
                  _resident(blk), _resident((1, D_MODEL)), _resident((1, D_MODEL))],
        out_specs=[row_spec, state_spec, state_spec],
        out_shape=[jax.ShapeDtypeStruct((n, D_MODEL), BF16),
                   jax.ShapeDtypeStruct((nb, SUBLANES, D_MODEL), F32),
                   jax.ShapeDtypeStruct((nb, SUBLANES, D_MODEL), F32)],
        scratch_shapes=[pltpu.VMEM((tc + SUBLANES, D_MODEL), F32),
                        pltpu.VMEM((tc, D_MODEL), F32),
                        pltpu.VMEM((tc, D_MODEL), F32),
                        pltpu.VMEM((SUBLANES, D_MODEL), F32)],
        compiler_params=pltpu.CompilerParams(
            dimension_semantics=("arbitrary", "arbitrary"), vmem_limit_bytes=VMEM_LIMIT),
        name="rglru",
    )(xr, gr, h0, hist, cw, cb, wa, ba, wx, bx, lam)


def _attn_kernel(lamp_ref, q_ref, k_ref, v_ref, km_ref, vm_ref, g_ref, o_ref,
                 qq_ref, ok_ref, *, tq, nq):
    i = pl.program_id(2)
    th = tq // 2
    q = q_ref[...]
    lane = lax.broadcasted_iota(jnp.int32, (1, LANES), 1)
    zero = jnp.zeros_like(q)
    q1 = jnp.where(lane < HEAD_DIM, q, zero)
    q2 = jnp.where(lane >= HEAD_DIM, q, zero)
    qq_ref[...] = jnp.concatenate([q1[:th], q2[:th], q1[th:], q2[th:]], axis=0)
    diag_off = pl.multiple_of(i * tq, tq)

    def scores(lhs, kt):
        return lax.dot_general(lhs, kt, (((1,), (1,)), ((), ())),
                               preferred_element_type=F32)

    def unstack(x):
        return (jnp.concatenate([x[0:th], x[2 * th:3 * th]], axis=0),
                jnp.concatenate([x[th:2 * th], x[3 * th:]], axis=0))

    def finish(l, acc):
        lp = lamp_ref[...]
        lam = (jnp.exp(jnp.sum(lp[0:1] * lp[1:2], axis=-1, keepdims=True))
               - jnp.exp(jnp.sum(lp[2:3] * lp[3:4], axis=-1, keepdims=True)) + LAM_INIT)
        (l1, l2), (a1, a2) = unstack(l), unstack(acc)
        o = a1 / l1 - lam * (a2 / l2)
        ms = jnp.mean(o * o, axis=-1, keepdims=True)
        return o * lax.rsqrt(ms + EPS) * g_ref[...] * (1.0 - LAM_INIT), ms

    def row_sum(x):
        return jnp.broadcast_to(jnp.sum(x, axis=-1, keepdims=True), x.shape)

    def exp_parts(s, m):
        parts = [jnp.exp2(s[:, c * LANES:(c + 1) * LANES] - m)
                 for c in range(s.shape[1] // LANES)]
        lsum = parts[0]
        for pc in parts[1:]:
            lsum = lsum + pc
        p = jnp.concatenate(parts, axis=1) if len(parts) > 1 else parts[0]
        return p.astype(BF16), lsum

    def pv(p, vt):
        return jnp.dot(p, vt, preferred_element_type=F32)

    def causal(s):
        qpos = lax.broadcasted_iota(jnp.int32, s.shape, 0) & (th - 1)
        kpos = lax.broadcasted_iota(jnp.int32, s.shape, 1) + (th - s.shape[1])
        return jnp.where(kpos <= qpos, s, MASK_VALUE)

    def fast_step(c):
        qq = qq_ref[...]
        s_meta = jnp.where(lane < N_META, scores(qq, km_ref[...]), MASK_VALUE)
        m = jnp.broadcast_to(jnp.max(s_meta, axis=-1, keepdims=True), (2 * tq, LANES))
        p_meta, l_meta = exp_parts(s_meta, m)

        d0 = c * tq
        s_top = causal(scores(qq[0:tq], k_ref[d0:d0 + th, :]))
        s_bot = causal(scores(qq[tq:2 * tq], k_ref[d0:d0 + tq, :]))
        p_top, l_top = exp_parts(s_top, m[0:tq])
        p_bot, l_bot = exp_parts(s_bot, m[tq:2 * tq])
        vm = vm_ref[...]
        l_top = l_meta[0:tq] + l_top
        l_bot = l_meta[tq:2 * tq] + l_bot
        acc_top = pv(p_meta[0:tq], vm) + pv(p_top, v_ref[d0:d0 + th, :])
        acc_bot = pv(p_meta[tq:2 * tq], vm) + pv(p_bot, v_ref[d0:d0 + tq, :])

        for off in range(0, d0, KV_RUN):
            n = min(KV_RUN, d0 - off)
            p, lsum = exp_parts(scores(qq, k_ref[off:off + n, :]), m)
            acc = pv(p, v_ref[off:off + n, :])
            l_top, l_bot = l_top + lsum[0:tq], l_bot + lsum[tq:2 * tq]
            acc_top, acc_bot = acc_top + acc[0:tq], acc_bot + acc[tq:2 * tq]

        l = jnp.concatenate([row_sum(l_top), row_sum(l_bot)], axis=0)
        o, ms = finish(l, jnp.concatenate([acc_top, acc_bot], axis=0))
        o_ref[...] = o.astype(BF16)

        l1, l2 = unstack(l)
        l_ok = jnp.logical_and(l1 < SAFE_SUM, l2 < SAFE_SUM)
        ok_ref[0] = jnp.minimum(jnp.min(jnp.where(l_ok, 1.0, 0.0)),
                                jnp.min(jnp.where(ms < SAFE_OUT, 1.0, 0.0)))

    for c in range(nq):
        pl.when(i == c)(functools.partial(fast_step, c))

    @pl.when(ok_ref[0] < 0.5)
    def _():
        def update(carry, s, vt):
            m, l, acc = carry
            m_new = jnp.maximum(m, jnp.max(s, axis=-1, keepdims=True))
            alpha = jnp.exp2(m - m_new)
            p = jnp.exp2(s - m_new)
            l = alpha * l + jnp.sum(p, axis=-1, keepdims=True)
            acc = alpha * acc + pv(p.astype(BF16), vt)
            return m_new, l, acc

        def safe_tile(j, carry):
            off = pl.multiple_of(j * tq, tq)
            return update(carry, scores(qq_ref[...], k_ref[pl.ds(off, tq), :]),
                          v_ref[pl.ds(off, tq), :])

        s0 = jnp.where(lane < N_META, scores(qq_ref[...], km_ref[...]), MASK_VALUE)
        m0 = jnp.max(s0, axis=-1, keepdims=True)
        p0 = jnp.exp2(s0 - m0)
        carry = (m0, jnp.sum(p0, axis=-1, keepdims=True), pv(p0.astype(BF16), vm_ref[...]))
        carry = lax.fori_loop(0, i, safe_tile, carry)
        s = scores(qq_ref[...], k_ref[pl.ds(diag_off, tq), :])
        row = lax.broadcasted_iota(jnp.int32, s.shape, 0)
        qpos = (row & (th - 1)) + jnp.where(row >= tq, th, 0)
        kpos = lax.broadcasted_iota(jnp.int32, s.shape, 1)
        s = jnp.where(kpos <= qpos, s, MASK_VALUE)
        _, l, acc = update(carry, s, v_ref[pl.ds(diag_off, tq), :])
        o_ref[...] = finish(l, acc)[0].astype(BF16)


def _attention(lamp, q, k, v, km, vm, g_subln, *, nb, seq, tq):
    nq = seq // tq
    q_spec = pl.BlockSpec((tq, V_DIM), lambda b, h, i: (b * nq + i, h))
    kv_spec = pl.BlockSpec((seq, V_DIM), lambda b, h, i: (b, h))
    meta_spec = pl.BlockSpec((LANES, V_DIM), lambda b, h, i: (0, h))
    return pl.pallas_call(
        functools.partial(_attn_kernel, tq=tq, nq=nq),
        grid=(nb, N_HEADS, nq),
        in_specs=[pl.BlockSpec((4, HEAD_DIM), lambda b, h, i: (0, 0)),
                  q_spec, kv_spec, kv_spec, meta_spec, meta_spec,
                  pl.BlockSpec((1, V_DIM), lambda b, h, i: (0, 0))],
        out_specs=q_spec,
        out_shape=jax.ShapeDtypeStruct(q.shape, BF16),
        scratch_shapes=[pltpu.VMEM((2 * tq, LANES), BF16),
                        pltpu.SMEM((1,), F32)],
        compiler_params=pltpu.CompilerParams(
            dimension_semantics=("arbitrary", "arbitrary", "arbitrary"),
            vmem_limit_bytes=VMEM_LIMIT),
        name="diffattn",
    )(lamp, q, k, v, km, vm, g_subln)


def _merge_kernel(x_ref, y_ref, o_ref, sgr_ref, sga_ref, wr_ref, wa_ref, wo_ref, h_ref):
    y_r = jnp.dot(y_ref[...], wr_ref[...], preferred_element_type=F32)
    y_a = jnp.dot(o_ref[...], wa_ref[...], preferred_element_type=F32)
    merged = sgr_ref[...].astype(F32) * y_r + sga_ref[...].astype(F32) * y_a
    h_ref[...] = x_ref[...] + jnp.dot(merged.astype(BF16), wo_ref[...],
                                      preferred_element_type=F32)


def _merge(x2d, y, o, sgr, sga, wr, wa, wo, *, tm):
    n = x2d.shape[0]
    row_spec = pl.BlockSpec((tm, D_MODEL), lambda i: (i, 0))
    w_spec = _resident((D_MODEL, D_MODEL))
    return pl.pallas_call(
        _merge_kernel,
        grid=(n // tm,),
        in_specs=[row_spec] * 5 + [w_spec] * 3,
        out_specs=row_spec,
        out_shape=jax.ShapeDtypeStruct((n, D_MODEL), F32),
        compiler_params=pltpu.CompilerParams(
            dimension_semantics=("arbitrary",), vmem_limit_bytes=VMEM_LIMIT),
        name="merge",
    )(x2d, y, o, sgr, sga, wr, wa, wo)


def _mlp_kernel(h_ref, g_ref, w1_ref, w2_ref, gf_ref, out_ref):
    h = h_ref[...]
    hn = (h * _rms_scale(h) * g_ref[...]).astype(BF16)
    acc = h
    for c in range(D_FF // D_MODEL):
        ff = jnp.dot(hn, w1_ref[:, c * D_MODEL:(c + 1) * D_MODEL],
                     preferred_element_type=F32)
        ff = jnp.square(jnp.maximum(ff, 0.0)).astype(BF16)
        acc = acc + jnp.dot(ff, w2_ref[c * D_MODEL:(c + 1) * D_MODEL, :],
                            preferred_element_type=F32)
    out_ref[...] = acc * _rms_scale(acc) * gf_ref[...]


def _mlp(h, g, w1, w2, gf, *, tm):
    n = h.shape[0]
    row_spec = pl.BlockSpec((tm, D_MODEL), lambda i: (i, 0))
    return pl.pallas_call(
        _mlp_kernel,
        grid=(n // tm,),
        in_specs=[row_spec, _resident((1, D_MODEL)), _resident((D_MODEL, D_FF)),
                  _resident((D_FF, D_MODEL)), _resident((1, D_MODEL))],
        out_specs=row_spec,
        out_shape=jax.ShapeDtypeStruct((n, D_MODEL), F32),
        compiler_params=pltpu.CompilerParams(
            dimension_semantics=("arbitrary",), vmem_limit_bytes=VMEM_LIMIT),
        name="mlp",
    )(h, g, w1, w2, gf)


def _rope_tables(n_pos):
    inv = 1.0 / (ROPE_THETA ** (jnp.arange(0, HEAD_DIM, 2, dtype=F32) / HEAD_DIM))
    ang = jnp.arange(n_pos, dtype=F32)[:, None] * inv[None, :]
    cos, sin = jnp.cos(ang), jnp.sin(ang)
    reps = LANES // HEAD_DIM
    return (jnp.tile(cos, (1, 2 * reps)),
            jnp.tile(jnp.concatenate([-sin, sin], axis=1), (1, reps)))


def kernel(x, meta_tokens, g_mix, w_in, conv_w, conv_b, w_a, b_a, w_x, b_x, lru_lambda,
           lam_q1, lam_k1, lam_q2, lam_k2, g_subln, w_rnn_out, w_attn_out, w_o,
           g_mlp, w_ff1, w_ff2, g_final):
    nb, seq, _ = x.shape
    tm = 512
    tc = 512
    tq = 512
    x2d = x.reshape(nb * seq, D_MODEL)
    row = lambda p: p.reshape(1, -1).astype(F32)

    cos, sin = _rope_tables(N_META + seq)
    q_scale = math.log2(math.e) / math.sqrt(HEAD_DIM)
    w_in_b = w_in[0].astype(BF16)
    g_mix_r = row(g_mix[0])

    inproj = functools.partial(_inproj, q_scale=q_scale)
    xr_m, gr_m, _, k_m, v_m, _, _ = inproj(
        meta_tokens.astype(F32), g_mix_r, w_in_b, cos[:N_META], sin[:N_META],
        tm=N_META, pos_tiles=1)
    xr, gr, q, k, v, sgr, sga = inproj(
        x2d, g_mix_r, w_in_b, cos[N_META:], sin[N_META:], tm=tm, pos_tiles=seq // tm)

    rnn_params = (conv_w[0].astype(F32), row(conv_b[0]), (0.5 * w_a[0]).astype(BF16),
                  row(0.5 * b_a[0]), (0.5 * w_x[0]).astype(BF16), row(0.5 * b_x[0]),
                  row(lru_lambda[0]))
    zeros = jnp.zeros((SUBLANES, D_MODEL), F32)
    _, h_meta, xtail_meta = _rglru(xr_m, gr_m, zeros, zeros, *rnn_params, nb=1, tc=N_META)
    y, _, _ = _rglru(xr, gr, h_meta[0], xtail_meta[0], *rnn_params, nb=nb, tc=tc)

    lamp = jnp.stack([lam_q1[0], lam_k1[0], lam_q2[0], lam_k2[0]]).astype(F32)
    pad = ((0, LANES - N_META), (0, 0))
    o = _attention(lamp, q, k, v, jnp.pad(k_m, pad), jnp.pad(v_m, pad), row(g_subln[0]),
                   nb=nb, seq=seq, tq=tq)

    h1 = _merge(x2d, y, o, sgr, sga, w_rnn_out[0].astype(BF16),
                w_attn_out[0].astype(BF16), w_o[0].astype(BF16), tm=tm)
    out = _mlp(h1, row(g_mlp[0]), w_ff1[0].astype(BF16), w_ff2[0].astype(BF16),
               row(g_final), tm=tm)
    return out.reshape(nb, seq, D_MODEL)
```

```python
import functools
import math

import jax
import jax.numpy as jnp
import numpy as np
from jax import lax
from jax.experimental import pallas as pl
from jax.experimental.pallas import tpu as pltpu

D_MODEL = 1024
N_META = 16
N_RNN_BLOCKS = 8
RNN_BLOCK = D_MODEL // N_RNN_BLOCKS
CONV_W = 4
LRU_C = 8.0
N_HEADS = 8
HEAD_DIM = 64
V_DIM = 128
ROPE_THETA = 10000.0
D_FF = 4 * D_MODEL
EPS = 1e-6
N_IN_GROUPS = 7
LAM_INIT = 0.8 - 0.6 * math.exp(-0.3 * 0)

LANES = 128
SUBLANES = 8
VMEM_LIMIT = 56 * 1024 * 1024
MASK_VALUE = -1e30
LARGE_SUM = 2.0 ** 100
SMALL_SUM = 2.0 ** -60
SAFE_OUT = 3.0e38
KV_RUN = 2048

BF16 = jnp.bfloat16
F32 = jnp.float32


def _f32_bits(v):
    return int(np.float32(v).view(np.int32))


def _sigmoid(x):
    return 0.5 * jnp.tanh(0.5 * x) + 0.5


def _gelu_tanh(x):
    c = math.sqrt(2.0 / math.pi)
    return (0.5 * x) * (1.0 + jnp.tanh(x * ((c * 0.044715) * (x * x) + c)))


def _rms_scale(xf):
    return lax.rsqrt(jnp.mean(xf * xf, axis=-1, keepdims=True) + EPS)


def _resident(shape):
    nd = len(shape)
    return pl.BlockSpec(shape, lambda *_: (0,) * nd, pipeline_mode=pl.Buffered(1))


def _inproj_kernel(x_ref, g_ref, w_ref, cos_ref, sin_ref,
                   xr_ref, gr_ref, q_ref, k_ref, v_ref, sgr_ref, sga_ref, *, q_scale):
    xf = x_ref[...]
    hn = (xf * _rms_scale(xf) * g_ref[...]).astype(BF16)
    cos = cos_ref[...]
    sin = sin_ref[...]
    lane = lax.broadcasted_iota(jnp.int32, (1, LANES), 1)
    first_half = (lane & (HEAD_DIM // 2)) == 0

    def proj(c):
        return jnp.dot(hn, w_ref[:, c * D_MODEL:(c + 1) * D_MODEL],
                       preferred_element_type=F32)

    def rope_store(r, out_ref, scale):
        for j in range(D_MODEL // LANES):
            xg = r[:, j * LANES:(j + 1) * LANES]
            rot = jnp.where(first_half,
                            pltpu.roll(xg, LANES - HEAD_DIM // 2, 1),
                            pltpu.roll(xg, HEAD_DIM // 2, 1))
            o = xg * cos + rot * sin
            if scale != 1.0:
                o = o * scale
            out_ref[:, j * LANES:(j + 1) * LANES] = o.astype(BF16)

    xr_ref[...] = proj(0).astype(BF16)
    gr_ref[...] = proj(1).astype(BF16)
    rope_store(proj(2), q_ref, q_scale)
    rope_store(proj(3), k_ref, 1.0)
    v_ref[...] = proj(4).astype(BF16)
    sgr_ref[...] = _sigmoid(proj(5)).astype(BF16)
    sga_ref[...] = _sigmoid(proj(6)).astype(BF16)


def _inproj(x2d, g, w_in, cos, sin, *, tm, pos_tiles, q_scale):
    n = x2d.shape[0]
    row_spec = pl.BlockSpec((tm, D_MODEL), lambda i: (i, 0))
    tab_spec = pl.BlockSpec((tm, LANES), lambda i: (i % pos_tiles, 0))
    out = jax.ShapeDtypeStruct((n, D_MODEL), BF16)
    return pl.pallas_call(
        functools.partial(_inproj_kernel, q_scale=q_scale),
        grid=(n // tm,),
        in_specs=[row_spec, _resident((1, D_MODEL)),
                  _resident((D_MODEL, N_IN_GROUPS * D_MODEL)), tab_spec, tab_spec],
        out_specs=[row_spec] * N_IN_GROUPS,
        out_shape=[out] * N_IN_GROUPS,
        compiler_params=pltpu.CompilerParams(
            dimension_semantics=("arbitrary",), vmem_limit_bytes=VMEM_LIMIT),
        name="inproj",
    )(x2d, g, w_in, cos, sin)


def _rglru_kernel(xr_ref, gr_ref, h0_ref, hist_ref, cw_ref, cb_ref, wa_ref, ba_ref,
                  wx_ref, bx_ref, lam_ref, y_ref, hlast_ref, xtail_ref,
                  ext_ref, a_ref, u_ref, h_ref, *, tc):
    t = pl.program_id(1)

    @pl.when(t == 0)
    def _():
        ext_ref[0:SUBLANES, :] = hist_ref[...]
        h_ref[...] = h0_ref[...]

    x = xr_ref[...].astype(F32)
    ext_ref[SUBLANES:SUBLANES + tc, :] = x
    xc = cb_ref[...] + cw_ref[CONV_W - 1:CONV_W, :] * x
    for j in range(1, CONV_W):
        xc = xc + (cw_ref[CONV_W - 1 - j:CONV_W - j, :]
                   * ext_ref[SUBLANES - j:SUBLANES - j + tc, :])
    ext_ref[0:SUBLANES, :] = x[tc - SUBLANES:tc, :]

    xcb = xc.astype(BF16)

    def gate_tanh(w_ref, b_ref):
        parts = [jnp.dot(xcb[:, n * RNN_BLOCK:(n + 1) * RNN_BLOCK], w_ref[n],
                         preferred_element_type=F32) for n in range(N_RNN_BLOCKS)]
        return jnp.tanh(jnp.concatenate(parts, axis=1) + b_ref[...])

    tr = gate_tanh(wa_ref, ba_ref)
    ti = gate_tanh(wx_ref, bx_ref)
    z = -lam_ref[...]
    softplus = jnp.maximum(z, 0.0) + jnp.log1p(jnp.exp(-jnp.abs(z)))
    half_rate = (-0.5 * LRU_C) * softplus
    log_a = half_rate * tr + half_rate
    a = jnp.exp(log_a)
    th = jnp.tanh(log_a)
    root = jnp.exp2(0.5 * jnp.log2(-0.5 * th / (1.0 - th)))
    u = root * ((ti + 1.0) * xc)

    nt = tc // SUBLANES
    a3 = a.reshape(nt, SUBLANES, D_MODEL)
    u3 = u.reshape(nt, SUBLANES, D_MODEL)
    row = lax.broadcasted_iota(jnp.int32, (1, SUBLANES, D_MODEL), 1)
    k = 1
    while k < SUBLANES:
        keep = row >= k
        u3 = u3 + a3 * jnp.where(keep, pltpu.roll(u3, k, 1), 0.0)
        a3 = a3 * jnp.where(keep, pltpu.roll(a3, k, 1), 1.0)
        k *= 2
    a_ref[...] = a3.reshape(tc, D_MODEL)
    u_ref[...] = u3.reshape(tc, D_MODEL)

    def tile_step(i, carry):
        off = pl.multiple_of(i * SUBLANES, SUBLANES)
        h = u_ref[pl.ds(off, SUBLANES), :] + a_ref[pl.ds(off, SUBLANES), :] * carry
        u_ref[pl.ds(off, SUBLANES), :] = h
        return jnp.broadcast_to(h[SUBLANES - 1:SUBLANES, :], (SUBLANES, D_MODEL))

    carry = lax.fori_loop(0, nt, tile_step, h_ref[...])
    h_ref[...] = carry

    y_ref[...] = (_gelu_tanh(gr_ref[...].astype(F32)) * u_ref[...]).astype(BF16)
    hlast_ref[0] = carry
    xtail_ref[0] = x[tc - SUBLANES:tc, :]


def _rglru(xr, gr, h0, hist, cw, cb, wa, ba, wx, bx, lam, *, nb, tc):
    n = xr.shape[0]
    nchunk = n // (nb * tc)
    row_spec = pl.BlockSpec((tc, D_MODEL), lambda b, t: (b * nchunk + t, 0))
    state_spec = pl.BlockSpec((1, SUBLANES, D_MODEL), lambda b, t: (b, 0, 0))
    blk = (N_RNN_BLOCKS, RNN_BLOCK, RNN_BLOCK)
    return pl.pallas_call(
        functools.partial(_rglru_kernel, tc=tc),
        grid=(nb, nchunk),
        in_specs=[row_spec, row_spec, _resident((SUBLANES, D_MODEL)),
                  _resident((SUBLANES, D_MODEL)), _resident((CONV_W, D_MODEL)),
                  _resident((1, D_MODEL)), _resident(blk), _resident((1, D_MODEL)),
                  _resident(blk), _resident((1, D_MODEL)), _resident((1, D_MODEL))],
        out_specs=[row_spec, state_spec, state_spec],
        out_shape=[jax.ShapeDtypeStruct((n, D_MODEL), BF16),
                   jax.ShapeDtypeStruct((nb, SUBLANES, D_MODEL), F32),
                   jax.ShapeDtypeStruct((nb, SUBLANES, D_MODEL), F32)],
        scratch_shapes=[pltpu.VMEM((tc + SUBLANES, D_MODEL), F32),
                        pltpu.VMEM((tc, D_MODEL), F32),
                        pltpu.VMEM((tc, D_MODEL), F32),
                        pltpu.VMEM((SUBLANES, D_MODEL), F32)],
        compiler_params=pltpu.CompilerParams(
            dimension_semantics=("arbitrary", "arbitrary"), vmem_limit_bytes=VMEM_LIMIT),
        name="rglru",
    )(xr, gr, h0, hist, cw, cb, wa, ba, wx, bx, lam)


def _attn_kernel(lamp_ref, q_ref, k_ref, v_ref, km_ref, vm_ref, g_ref, o_ref,
                 qq_ref, ok_ref, *, tq, nq, tps):
    i = pl.program_id(2)
    th = tq // 2
    lane = lax.broadcasted_iota(jnp.int32, (1, LANES), 1)
    for t in range(tps):
        q = q_ref[t * tq:(t + 1) * tq, :]
        zero = jnp.zeros_like(q)
        q1 = jnp.where(lane < HEAD_DIM, q, zero)
        q2 = jnp.where(lane >= HEAD_DIM, q, zero)
        qq_ref[2 * t * tq:2 * (t + 1) * tq, :] = jnp.concatenate(
            [q1[:th], q2[:th], q1[th:], q2[th:]], axis=0)

    def scores(lhs, kt):
        return lax.dot_general(lhs, kt, (((1,), (1,)), ((), ())),
                               preferred_element_type=F32)

    def unstack(x):
        return (jnp.concatenate([x[0:th], x[2 * th:3 * th]], axis=0),
                jnp.concatenate([x[th:2 * th], x[3 * th:]], axis=0))

    def finish(l, acc):
        lp = lamp_ref[...]
        lam = (jnp.exp(jnp.sum(lp[0:1] * lp[1:2], axis=-1, keepdims=True))
               - jnp.exp(jnp.sum(lp[2:3] * lp[3:4], axis=-1, keepdims=True)) + LAM_INIT)
        (l1, l2), (a1, a2) = unstack(l), unstack(acc)
        o = a1 / l1 - lam * (a2 / l2)
        ms = jnp.mean(o * o, axis=-1, keepdims=True)
        return o * lax.rsqrt(ms + EPS) * (g_ref[...] * (1.0 - LAM_INIT)), ms

    def exp_tile(s):
        return jnp.exp2(s).astype(BF16)

    def pv(p, vt):
        return jnp.dot(p, vt, preferred_element_type=F32)

    def pv_sums(p, vt):
        ones = jnp.ones((vt.shape[0], LANES), BF16)
        return pv(p, jnp.concatenate([vt, ones], axis=1))

    def causal(s):
        qpos = lax.broadcasted_iota(jnp.int32, s.shape, 0) & (th - 1)
        kpos = lax.broadcasted_iota(jnp.int32, s.shape, 1) + (th - s.shape[1])
        return jnp.where(kpos <= qpos, s, MASK_VALUE)

    def fast_tile(c, t):
        qq = qq_ref[2 * t * tq:2 * (t + 1) * tq, :]
        p_meta = exp_tile(jnp.where(lane < N_META, scores(qq, km_ref[...]), MASK_VALUE))

        d0 = c * tq
        p_top = exp_tile(causal(scores(qq[0:tq], k_ref[d0:d0 + th, :])))
        p_bot = exp_tile(causal(scores(qq[tq:2 * tq], k_ref[d0:d0 + tq, :])))
        vm = vm_ref[...]
        acc_top = pv_sums(p_meta[0:tq], vm) + pv_sums(p_top, v_ref[d0:d0 + th, :])
        acc_bot = pv_sums(p_meta[tq:2 * tq], vm) + pv_sums(p_bot, v_ref[d0:d0 + tq, :])

        for off in range(0, d0, KV_RUN):
            n = min(KV_RUN, d0 - off)
            acc = pv_sums(exp_tile(scores(qq, k_ref[off:off + n, :])), v_ref[off:off + n, :])
            acc_top, acc_bot = acc_top + acc[0:tq], acc_bot + acc[tq:2 * tq]

        acc = jnp.concatenate([acc_top, acc_bot], axis=0)
        l = acc[:, V_DIM:]
        o, ms = finish(l, acc[:, :V_DIM])
        o_ref[t * tq:(t + 1) * tq, :] = o.astype(BF16)

        def bits_min_max(x):
            b = pltpu.bitcast(x, jnp.int32).reshape(-1, SUBLANES, x.shape[1])
            return jnp.min(b, axis=0), jnp.max(b, axis=0)

        l_lo, l_hi = bits_min_max(l)
        ms_lo, ms_hi = bits_min_max(ms)
        l_ok = jnp.logical_and(l_lo > _f32_bits(SMALL_SUM), l_hi < _f32_bits(LARGE_SUM))
        ms_ok = jnp.logical_and(ms_lo >= 0, ms_hi < _f32_bits(SAFE_OUT))
        ok_ref[t] = jnp.minimum(jnp.min(jnp.where(l_ok, 1.0, 0.0)),
                                jnp.min(jnp.where(ms_ok, 1.0, 0.0)))

    def fast_group(g):
        for t in range(tps):
            fast_tile(g * tps + t, t)

    for g in range(nq // tps):
        pl.when(i == g)(functools.partial(fast_group, g))

    def online_tile(t):
        c = i * tps + t
        qq = qq_ref[pl.ds(pl.multiple_of(2 * t * tq, 2 * tq), 2 * tq), :]
        diag_off = pl.multiple_of(c * tq, tq)

        def update(carry, s, vt):
            m, l, acc = carry
            m_new = jnp.maximum(m, jnp.max(s, axis=-1, keepdims=True))
            alpha = jnp.exp2(m - m_new)
            p = jnp.exp2(s - m_new)
            l = alpha * l + jnp.sum(p, axis=-1, keepdims=True)
            acc = alpha * acc + pv(p.astype(BF16), vt)
            return m_new, l, acc

        def full_tile(j, carry):
            off = pl.multiple_of(j * tq, tq)
            return update(carry, scores(qq, k_ref[pl.ds(off, tq), :]),
                          v_ref[pl.ds(off, tq), :])

        s0 = jnp.where(lane < N_META, scores(qq, km_ref[...]), MASK_VALUE)
        m0 = jnp.max(s0, axis=-1, keepdims=True)
        p0 = jnp.exp2(s0 - m0)
        carry = (m0, jnp.sum(p0, axis=-1, keepdims=True), pv(p0.astype(BF16), vm_ref[...]))
        carry = lax.fori_loop(0, c, full_tile, carry)
        s = scores(qq, k_ref[pl.ds(diag_off, tq), :])
        row = lax.broadcasted_iota(jnp.int32, s.shape, 0)
        qpos = (row & (th - 1)) + jnp.where(row >= tq, th, 0)
        kpos = lax.broadcasted_iota(jnp.int32, s.shape, 1)
        s = jnp.where(kpos <= qpos, s, MASK_VALUE)
        _, l, acc = update(carry, s, v_ref[pl.ds(diag_off, tq), :])
        o_ref[pl.ds(pl.multiple_of(t * tq, tq), tq), :] = finish(l, acc)[0].astype(BF16)

    def check_slot(t, _):
        pl.when(ok_ref[t] < 0.5)(functools.partial(online_tile, t))
        return 0

    lax.fori_loop(0, tps, check_slot, 0)


def _attention(lamp, q, k, v, km, vm, g_subln, *, nb, seq, tq, tps):
    nq = seq // tq
    ng = nq // tps
    q_spec = pl.BlockSpec((tps * tq, V_DIM), lambda b, h, i: (b * ng + i, h))
    kv_spec = pl.BlockSpec((seq, V_DIM), lambda b, h, i: (b, h))
    meta_spec = pl.BlockSpec((LANES, V_DIM), lambda b, h, i: (0, h))
    return pl.pallas_call(
        functools.partial(_attn_kernel, tq=tq, nq=nq, tps=tps),
        grid=(nb, N_HEADS, ng),
        in_specs=[pl.BlockSpec((4, HEAD_DIM), lambda b, h, i: (0, 0)),
                  q_spec, kv_spec, kv_spec, meta_spec, meta_spec,
                  pl.BlockSpec((1, V_DIM), lambda b, h, i: (0, 0))],
        out_specs=q_spec,
        out_shape=jax.ShapeDtypeStruct(q.shape, BF16),
        scratch_shapes=[pltpu.VMEM((2 * tps * tq, LANES), BF16),
                        pltpu.SMEM((tps,), F32)],
        compiler_params=pltpu.CompilerParams(
            dimension_semantics=("arbitrary", "arbitrary", "arbitrary"),
            vmem_limit_bytes=VMEM_LIMIT),
        name="diffattn",
    )(lamp, q, k, v, km, vm, g_subln)


def _merge_kernel(x_ref, y_ref, o_ref, sgr_ref, sga_ref, wr_ref, wa_ref, wo_ref, h_ref):
    y_r = jnp.dot(y_ref[...], wr_ref[...], preferred_element_type=F32)
    y_a = jnp.dot(o_ref[...], wa_ref[...], preferred_element_type=F32)
    merged = sgr_ref[...].astype(F32) * y_r + sga_ref[...].astype(F32) * y_a
    h_ref[...] = x_ref[...] + jnp.dot(merged.astype(BF16), wo_ref[...],
                                      preferred_element_type=F32)


def _merge(x2d, y, o, sgr, sga, wr, wa, wo, *, tm):
    n = x2d.shape[0]
    row_spec = pl.BlockSpec((tm, D_MODEL), lambda i: (i, 0))
    w_spec = _resident((D_MODEL, D_MODEL))
    return pl.pallas_call(
        _merge_kernel,
        grid=(n // tm,),
        in_specs=[row_spec] * 5 + [w_spec] * 3,
        out_specs=row_spec,
        out_shape=jax.ShapeDtypeStruct((n, D_MODEL), F32),
        compiler_params=pltpu.CompilerParams(
            dimension_semantics=("arbitrary",), vmem_limit_bytes=VMEM_LIMIT),
        name="merge",
    )(x2d, y, o, sgr, sga, wr, wa, wo)


def _mlp_kernel(h_ref, g_ref, w1_ref, w2_ref, gf_ref, out_ref):
    h = h_ref[...]
    hn = (h * _rms_scale(h) * g_ref[...]).astype(BF16)
    acc = h
    for c in range(D_FF // D_MODEL):
        ff = jnp.dot(hn, w1_ref[:, c * D_MODEL:(c + 1) * D_MODEL],
                     preferred_element_type=F32)
        ff = jnp.square(jnp.maximum(ff, 0.0)).astype(BF16)
        acc = acc + jnp.dot(ff, w2_ref[c * D_MODEL:(c + 1) * D_MODEL, :],
                            preferred_element_type=F32)
    out_ref[...] = acc * _rms_scale(acc) * gf_ref[...]


def _mlp(h, g, w1, w2, gf, *, tm):
    n = h.shape[0]
    row_spec = pl.BlockSpec((tm, D_MODEL), lambda i: (i, 0))
    return pl.pallas_call(
        _mlp_kernel,
        grid=(n // tm,),
        in_specs=[row_spec, _resident((1, D_MODEL)), _resident((D_MODEL, D_FF)),
                  _resident((D_FF, D_MODEL)), _resident((1, D_MODEL))],
        out_specs=row_spec,
        out_shape=jax.ShapeDtypeStruct((n, D_MODEL), F32),
        compiler_params=pltpu.CompilerParams(
            dimension_semantics=("arbitrary",), vmem_limit_bytes=VMEM_LIMIT),
        name="mlp",
    )(h, g, w1, w2, gf)


def _rope_tables(n_pos):
    inv = 1.0 / (ROPE_THETA ** (jnp.arange(0, HEAD_DIM, 2, dtype=F32) / HEAD_DIM))
    ang = jnp.arange(n_pos, dtype=F32)[:, None] * inv[None, :]
    cos, sin = jnp.cos(ang), jnp.sin(ang)
    reps = LANES // HEAD_DIM
    return (jnp.tile(cos, (1, 2 * reps)),
            jnp.tile(jnp.concatenate([-sin, sin], axis=1), (1, reps)))


def kernel(x, meta_tokens, g_mix, w_in, conv_w, conv_b, w_a, b_a, w_x, b_x, lru_lambda,
           lam_q1, lam_k1, lam_q2, lam_k2, g_subln, w_rnn_out, w_attn_out, w_o,
           g_mlp, w_ff1, w_ff2, g_final):
    nb, seq, _ = x.shape
    tm = 512
    tc = 512
    tq = 512
    tps = math.gcd(seq // tq, 8)
    x2d = x.reshape(nb * seq, D_MODEL)
    row = lambda p: p.reshape(1, -1).astype(F32)

    cos, sin = _rope_tables(N_META + seq)
    q_scale = math.log2(math.e) / math.sqrt(HEAD_DIM)
    w_in_b = w_in[0].astype(BF16)
    g_mix_r = row(g_mix[0])

    inproj = functools.partial(_inproj, q_scale=q_scale)
    xr_m, gr_m, _, k_m, v_m, _, _ = inproj(
        meta_tokens.astype(F32), g_mix_r, w_in_b, cos[:N_META], sin[:N_META],
        tm=N_META, pos_tiles=1)
    xr, gr, q, k, v, sgr, sga = inproj(
        x2d, g_mix_r, w_in_b, cos[N_META:], sin[N_META:], tm=tm, pos_tiles=seq // tm)

    rnn_params = (conv_w[0].astype(F32), row(conv_b[0]), (0.5 * w_a[0]).astype(BF16),
                  row(0.5 * b_a[0]), (0.5 * w_x[0]).astype(BF16), row(0.5 * b_x[0]),
                  row(lru_lambda[0]))
    zeros = jnp.zeros((SUBLANES, D_MODEL), F32)
    _, h_meta, xtail_meta = _rglru(xr_m, gr_m, zeros, zeros, *rnn_params, nb=1, tc=N_META)
    y, _, _ = _rglru(xr, gr, h_meta[0], xtail_meta[0], *rnn_params, nb=nb, tc=tc)

    lamp = jnp.stack([lam_q1[0], lam_k1[0], lam_q2[0], lam_k2[0]]).astype(F32)
    pad = ((0, LANES - N_META), (0, 0))
    o = _attention(lamp, q, k, v, jnp.pad(k_m, pad), jnp.pad(v_m, pad), row(g_subln[0]),
                   nb=nb, seq=seq, tq=tq, tps=tps)

    h1 = _merge(x2d, y, o, sgr, sga, w_rnn_out[0].astype(BF16),
                w_attn_out[0].astype(BF16), w_o[0].astype(BF16), tm=tm)
    out = _mlp(h1, row(g_mlp[0]), w_ff1[0].astype(BF16), w_ff2[0].astype(BF16),
               row(g_final), tm=tm)
    return out.reshape(nb, seq, D_MODEL)
```

```python
import functools
import math

import jax
import jax.numpy as jnp
import numpy as np
from jax import lax
from jax.experimental import pallas as pl
from jax.experimental.pallas import tpu as pltpu

D_MODEL = 1024
N_META = 16
N_RNN_BLOCKS = 8
RNN_BLOCK = D_MODEL // N_RNN_BLOCKS
CONV_W = 4
LRU_C = 8.0
N_HEADS = 8
HEAD_DIM = 64
V_DIM = 128
ROPE_THETA = 10000.0
D_FF = 4 * D_MODEL
EPS = 1e-6
N_IN_GROUPS = 7
LAM_INIT = 0.8 - 0.6 * math.exp(-0.3 * 0)

LANES = 128
SUBLANES = 8
N_SEG = SUBLANES


def _seg_pitch(tc):
    return tc // N_SEG + SUBLANES
VMEM_LIMIT = 56 * 1024 * 1024
MASK_VALUE = -1e30
LARGE_SUM = 2.0 ** 100
SMALL_SUM = 2.0 ** -60
SAFE_OUT = 3.0e38
KV_RUN = 2048

BF16 = jnp.bfloat16
F32 = jnp.float32


def _f32_bits(v):
    return int(np.float32(v).view(np.int32))


def _sigmoid(x):
    return 0.5 * jnp.tanh(0.5 * x) + 0.5


def _gelu_tanh(x):
    c = math.sqrt(2.0 / math.pi)
    return (0.5 * x) * (1.0 + jnp.tanh(x * ((c * 0.044715) * (x * x) + c)))


def _rms_scale(xf):
    return lax.rsqrt(jnp.mean(xf * xf, axis=-1, keepdims=True) + EPS)


def _resident(shape):
    nd = len(shape)
    return pl.BlockSpec(shape, lambda *_: (0,) * nd, pipeline_mode=pl.Buffered(1))


def _inproj_kernel(x_ref, g_ref, w_ref, cos_ref, sin_ref,
                   xr_ref, gr_ref, q_ref, k_ref, v_ref, sgr_ref, sga_ref, *, q_scale):
    xf = x_ref[...]
    hn = (xf * _rms_scale(xf) * g_ref[...]).astype(BF16)
    cos = cos_ref[...]
    sin = sin_ref[...]
    lane = lax.broadcasted_iota(jnp.int32, (1, LANES), 1)
    first_half = (lane & (HEAD_DIM // 2)) == 0

    def proj(c):
        return jnp.dot(hn, w_ref[:, c * D_MODEL:(c + 1) * D_MODEL],
                       preferred_element_type=F32)

    def rope_store(r, out_ref, scale):
        for j in range(D_MODEL // LANES):
            xg = r[:, j * LANES:(j + 1) * LANES]
            rot = jnp.where(first_half,
                            pltpu.roll(xg, LANES - HEAD_DIM // 2, 1),
                            pltpu.roll(xg, HEAD_DIM // 2, 1))
            o = xg * cos + rot * sin
            if scale != 1.0:
                o = o * scale
            out_ref[:, j * LANES:(j + 1) * LANES] = o.astype(BF16)

    xr_ref[...] = proj(0).astype(BF16)
    gr_ref[...] = proj(1).astype(BF16)
    rope_store(proj(2), q_ref, q_scale)
    rope_store(proj(3), k_ref, 1.0)
    v_ref[...] = proj(4).astype(BF16)
    sgr_ref[...] = _sigmoid(proj(5)).astype(BF16)
    sga_ref[...] = _sigmoid(proj(6)).astype(BF16)


def _inproj(x2d, g, w_in, cos, sin, *, tm, pos_tiles, q_scale):
    n = x2d.shape[0]
    row_spec = pl.BlockSpec((tm, D_MODEL), lambda i: (i, 0))
    tab_spec = pl.BlockSpec((tm, LANES), lambda i: (i % pos_tiles, 0))
    out = jax.ShapeDtypeStruct((n, D_MODEL), BF16)
    return pl.pallas_call(
        functools.partial(_inproj_kernel, q_scale=q_scale),
        grid=(n // tm,),
        in_specs=[row_spec, _resident((1, D_MODEL)),
                  _resident((D_MODEL, N_IN_GROUPS * D_MODEL)), tab_spec, tab_spec],
        out_specs=[row_spec] * N_IN_GROUPS,
        out_shape=[out] * N_IN_GROUPS,
        compiler_params=pltpu.CompilerParams(
            dimension_semantics=("arbitrary",), vmem_limit_bytes=VMEM_LIMIT),
        name="inproj",
    )(x2d, g, w_in, cos, sin)


def _lru_coeffs(xc, wa_ref, ba_ref, wx_ref, bx_ref, lam_ref):
    xcb = xc.astype(BF16)

    def gate_tanh(w_ref, b_ref):
        parts = [jnp.dot(xcb[:, n * RNN_BLOCK:(n + 1) * RNN_BLOCK], w_ref[n],
                         preferred_element_type=F32) for n in range(N_RNN_BLOCKS)]
        return jnp.tanh(jnp.concatenate(parts, axis=1) + b_ref[...])

    tr = gate_tanh(wa_ref, ba_ref)
    ti = gate_tanh(wx_ref, bx_ref)
    z = -lam_ref[...]
    softplus = jnp.maximum(z, 0.0) + jnp.log1p(jnp.exp(-jnp.abs(z)))
    half_rate = (-0.5 * LRU_C) * softplus
    log_a = half_rate * tr + half_rate
    a = jnp.exp(log_a)
    th = jnp.tanh(log_a)
    root = jnp.exp2(0.5 * jnp.log2(-0.5 * th / (1.0 - th)))
    return a, root * ((ti + 1.0) * xc)


def _rglru_kernel(xr_ref, gr_ref, h0_ref, hist_ref, cw_ref, cb_ref, wa_ref, ba_ref,
                  wx_ref, bx_ref, lam_ref, y_ref, hlast_ref, xtail_ref,
                  ext_ref, a_ref, u_ref, h_ref, *, tc):
    t = pl.program_id(1)

    @pl.when(t == 0)
    def _():
        ext_ref[0:SUBLANES, :] = hist_ref[...]
        h_ref[...] = h0_ref[...]

    x = xr_ref[...].astype(F32)
    ext_ref[SUBLANES:SUBLANES + tc, :] = x
    xc = cb_ref[...] + cw_ref[CONV_W - 1:CONV_W, :] * x
    for j in range(1, CONV_W):
        xc = xc + (cw_ref[CONV_W - 1 - j:CONV_W - j, :]
                   * ext_ref[SUBLANES - j:SUBLANES - j + tc, :])
    ext_ref[0:SUBLANES, :] = x[tc - SUBLANES:tc, :]

    a, u = _lru_coeffs(xc, wa_ref, ba_ref, wx_ref, bx_ref, lam_ref)

    nt = tc // SUBLANES
    a3 = a.reshape(nt, SUBLANES, D_MODEL)
    u3 = u.reshape(nt, SUBLANES, D_MODEL)
    row = lax.broadcasted_iota(jnp.int32, (1, SUBLANES, D_MODEL), 1)
    k = 1
    while k < SUBLANES:
        keep = row >= k
        u3 = u3 + a3 * jnp.where(keep, pltpu.roll(u3, k, 1), 0.0)
        a3 = a3 * jnp.where(keep, pltpu.roll(a3, k, 1), 1.0)
        k *= 2
    a_ref[...] = a3.reshape(tc, D_MODEL)
    u_ref[...] = u3.reshape(tc, D_MODEL)

    def tile_step(i, carry):
        off = pl.multiple_of(i * SUBLANES, SUBLANES)
        h = u_ref[pl.ds(off, SUBLANES), :] + a_ref[pl.ds(off, SUBLANES), :] * carry
        u_ref[pl.ds(off, SUBLANES), :] = h
        return jnp.broadcast_to(h[SUBLANES - 1:SUBLANES, :], (SUBLANES, D_MODEL))

    carry = lax.fori_loop(0, nt, tile_step, h_ref[...])
    h_ref[...] = carry

    y_ref[...] = (_gelu_tanh(gr_ref[...].astype(F32)) * u_ref[...]).astype(BF16)
    hlast_ref[0] = carry
    xtail_ref[0] = x[tc - SUBLANES:tc, :]


def _rglru(xr, gr, h0, hist, cw, cb, wa, ba, wx, bx, lam, *, nb, tc):
    n = xr.shape[0]
    nchunk = n // (nb * tc)
    row_spec = pl.BlockSpec((tc, D_MODEL), lambda b, t: (b * nchunk + t, 0))
    state_spec = pl.BlockSpec((1, SUBLANES, D_MODEL), lambda b, t: (b, 0, 0))
    blk = (N_RNN_BLOCKS, RNN_BLOCK, RNN_BLOCK)
    return pl.pallas_call(
        functools.partial(_rglru_kernel, tc=tc),
        grid=(nb, nchunk),
        in_specs=[row_spec, row_spec, _resident((SUBLANES, D_MODEL)),
                  _resident((SUBLANES, D_MODEL)), _resident((CONV_W, D_MODEL)),
                  _resident((1, D_MODEL)), _resident(blk), _resident((1, D_MODEL)),
                  _resident(blk), _resident((1, D_MODEL)), _resident((1, D_MODEL))],
        out_specs=[row_spec, state_spec, state_spec],
        out_shape=[jax.ShapeDtypeStruct((n, D_MODEL), BF16),
                   jax.ShapeDtypeStruct((nb, SUBLANES, D_MODEL), F32),
                   jax.ShapeDtypeStruct((nb, SUBLANES, D_MODEL), F32)],
        scratch_shapes=[pltpu.VMEM((tc + SUBLANES, D_MODEL), F32),
                        pltpu.VMEM((tc, D_MODEL), F32),
                        pltpu.VMEM((tc, D_MODEL), F32),
                        pltpu.VMEM((SUBLANES, D_MODEL), F32)],
        compiler_params=pltpu.CompilerParams(
            dimension_semantics=("arbitrary", "arbitrary"), vmem_limit_bytes=VMEM_LIMIT),
        name="rglru",
    )(xr, gr, h0, hist, cw, cb, wa, ba, wx, bx, lam)


def _rglru_seg_kernel(xr_ref, gr_ref, h0_ref, hist_ref, cw_ref, cb_ref, wa_ref, ba_ref,
                      wx_ref, bx_ref, lam_ref, y_ref, cvt_ref, tail_ref, h_ref, *, tc):
    t = pl.program_id(1)
    seg_len = tc // N_SEG
    pitch = _seg_pitch(tc)
    nslab = D_MODEL // LANES
    sub = lax.broadcasted_iota(jnp.int32, (SUBLANES, D_MODEL), 0)

    @pl.when(t == 0)
    def _():
        hist = hist_ref[...]
        for k in range(CONV_W - 1):
            tail_ref[k] = pltpu.roll(hist, k, 0)
        h_ref[...] = h0_ref[...]

    x = xr_ref[...].astype(F32)
    for g in range(nslab):
        for s in range(N_SEG):
            cvt_ref[g, s * pitch:s * pitch + seg_len, :] = (
                x[s * seg_len:(s + 1) * seg_len, g * LANES:(g + 1) * LANES])
    blocks = [jnp.concatenate([cvt_ref[g, pl.ds(j, N_SEG, stride=pitch), :]
                               for g in range(nslab)], axis=1) for j in range(seg_len)]

    def before(cur, prev):
        return jnp.where(sub == 0, pltpu.roll(prev, 1, 0), pltpu.roll(cur, 1, 0))

    early = [before(blocks[seg_len - 1 - k], tail_ref[k]) for k in range(CONV_W - 1)]
    for k in range(CONV_W - 1):
        tail_ref[k] = blocks[seg_len - 1 - k]

    xc = cb_ref[...] + cw_ref[CONV_W - 1:CONV_W, :] * jnp.concatenate(blocks, axis=0)
    for k in range(1, CONV_W):
        shifted = early[:k][::-1] + blocks[:seg_len - k]
        xc = xc + cw_ref[CONV_W - 1 - k:CONV_W - k, :] * jnp.concatenate(shifted, axis=0)

    a, u = _lru_coeffs(xc, wa_ref, ba_ref, wx_ref, bx_ref, lam_ref)

    h_loc, decay = [u[0:SUBLANES]], [a[0:SUBLANES]]
    for j in range(1, seg_len):
        aj = a[j * SUBLANES:(j + 1) * SUBLANES]
        h_loc.append(aj * h_loc[-1] + u[j * SUBLANES:(j + 1) * SUBLANES])
        decay.append(aj * decay[-1])

    c = h_ref[0:1, :]
    entering = []
    for s in range(N_SEG):
        entering.append(c)
        c = h_loc[-1][s:s + 1] + decay[-1][s:s + 1] * c
    h_ref[...] = jnp.broadcast_to(c, (SUBLANES, D_MODEL))
    h_in = jnp.concatenate(entering, axis=0)

    for j in range(seg_len):
        hj = h_loc[j] + decay[j] * h_in
        for g in range(nslab):
            cvt_ref[g, pl.ds(j, N_SEG, stride=pitch), :] = hj[:, g * LANES:(g + 1) * LANES]
    h = jnp.concatenate(
        [jnp.concatenate([cvt_ref[g, s * pitch:s * pitch + seg_len, :]
                          for g in range(nslab)], axis=1) for s in range(N_SEG)], axis=0)
    y_ref[...] = (_gelu_tanh(gr_ref[...].astype(F32)) * h).astype(BF16)


def _rglru_seg(xr, gr, h0, hist, cw, cb, wa, ba, wx, bx, lam, *, nb, tc):
    n = xr.shape[0]
    nchunk = n // (nb * tc)
    assert tc % (N_SEG * SUBLANES) == 0 and tc // N_SEG >= CONV_W
    row_spec = pl.BlockSpec((tc, D_MODEL), lambda b, t: (b * nchunk + t, 0))
    blk = (N_RNN_BLOCKS, RNN_BLOCK, RNN_BLOCK)
    return pl.pallas_call(
        functools.partial(_rglru_seg_kernel, tc=tc),
        grid=(nb, nchunk),
        in_specs=[row_spec, row_spec, _resident((SUBLANES, D_MODEL)),
                  _resident((SUBLANES, D_MODEL)), _resident((CONV_W, D_MODEL)),
                  _resident((1, D_MODEL)), _resident(blk), _resident((1, D_MODEL)),
                  _resident(blk), _resident((1, D_MODEL)), _resident((1, D_MODEL))],
        out_specs=row_spec,
        out_shape=jax.ShapeDtypeStruct((n, D_MODEL), BF16),
        scratch_shapes=[pltpu.VMEM((D_MODEL // LANES, N_SEG * _seg_pitch(tc), LANES), F32),
                        pltpu.VMEM((CONV_W - 1, SUBLANES, D_MODEL), F32),
                        pltpu.VMEM((SUBLANES, D_MODEL), F32)],
        compiler_params=pltpu.CompilerParams(
            dimension_semantics=("arbitrary", "arbitrary"), vmem_limit_bytes=VMEM_LIMIT),
        name="rglru_seg",
    )(xr, gr, h0, hist, cw, cb, wa, ba, wx, bx, lam)


def _attn_kernel(lamp_ref, q_ref, k_ref, v_ref, km_ref, vm_ref, g_ref, o_ref,
                 qq_ref, ok_ref, *, tq, nq, tps):
    i = pl.program_id(2)
    th = tq // 2
    lane = lax.broadcasted_iota(jnp.int32, (1, LANES), 1)
    for t in range(tps):
        q = q_ref[t * tq:(t + 1) * tq, :]
        zero = jnp.zeros_like(q)
        q1 = jnp.where(lane < HEAD_DIM, q, zero)
        q2 = jnp.where(lane >= HEAD_DIM, q, zero)
        qq_ref[2 * t * tq:2 * (t + 1) * tq, :] = jnp.concatenate(
            [q1[:th], q2[:th], q1[th:], q2[th:]], axis=0)

    def scores(lhs, kt):
        return lax.dot_general(lhs, kt, (((1,), (1,)), ((), ())),
                               preferred_element_type=F32)

    def unstack(x):
        return (jnp.concatenate([x[0:th], x[2 * th:3 * th]], axis=0),
                jnp.concatenate([x[th:2 * th], x[3 * th:]], axis=0))

    def finish(l, acc):
        lp = lamp_ref[...]
        lam = (jnp.exp(jnp.sum(lp[0:1] * lp[1:2], axis=-1, keepdims=True))
               - jnp.exp(jnp.sum(lp[2:3] * lp[3:4], axis=-1, keepdims=True)) + LAM_INIT)
        (l1, l2), (a1, a2) = unstack(l), unstack(acc)
        o = a1 / l1 - lam * (a2 / l2)
        ms = jnp.mean(o * o, axis=-1, keepdims=True)
        return o * lax.rsqrt(ms + EPS) * (g_ref[...] * (1.0 - LAM_INIT)), ms

    def exp_tile(s):
        return jnp.exp2(s).astype(BF16)

    def pv(p, vt):
        return jnp.dot(p, vt, preferred_element_type=F32)

    def pv_sums(p, vt):
        ones = jnp.ones((vt.shape[0], LANES), BF16)
        return pv(p, jnp.concatenate([vt, ones], axis=1))

    def causal(s):
        qpos = lax.broadcasted_iota(jnp.int32, s.shape, 0) & (th - 1)
        kpos = lax.broadcasted_iota(jnp.int32, s.shape, 1) + (th - s.shape[1])
        return jnp.where(kpos <= qpos, s, MASK_VALUE)

    def fast_tile(c, t):
        qq = qq_ref[2 * t * tq:2 * (t + 1) * tq, :]
        p_meta = exp_tile(jnp.where(lane < N_META, scores(qq, km_ref[...]), MASK_VALUE))

        d0 = c * tq
        p_top = exp_tile(causal(scores(qq[0:tq], k_ref[d0:d0 + th, :])))
        p_bot = exp_tile(causal(scores(qq[tq:2 * tq], k_ref[d0:d0 + tq, :])))
        vm = vm_ref[...]
        acc_top = pv_sums(p_meta[0:tq], vm) + pv_sums(p_top, v_ref[d0:d0 + th, :])
        acc_bot = pv_sums(p_meta[tq:2 * tq], vm) + pv_sums(p_bot, v_ref[d0:d0 + tq, :])

        for off in range(0, d0, KV_RUN):
            n = min(KV_RUN, d0 - off)
            acc = pv_sums(exp_tile(scores(qq, k_ref[off:off + n, :])), v_ref[off:off + n, :])
            acc_top, acc_bot = acc_top + acc[0:tq], acc_bot + acc[tq:2 * tq]

        acc = jnp.concatenate([acc_top, acc_bot], axis=0)
        l = acc[:, V_DIM:]
        o, ms = finish(l, acc[:, :V_DIM])
        o_ref[t * tq:(t + 1) * tq, :] = o.astype(BF16)

        def bits_min_max(x):
            b = pltpu.bitcast(x, jnp.int32).reshape(-1, SUBLANES, x.shape[1])
            return jnp.min(b, axis=0), jnp.max(b, axis=0)

        l_lo, l_hi = bits_min_max(l)
        ms_lo, ms_hi = bits_min_max(ms)
        l_ok = jnp.logical_and(l_lo > _f32_bits(SMALL_SUM), l_hi < _f32_bits(LARGE_SUM))
        ms_ok = jnp.logical_and(ms_lo >= 0, ms_hi < _f32_bits(SAFE_OUT))
        ok_ref[t] = jnp.minimum(jnp.min(jnp.where(l_ok, 1.0, 0.0)),
                                jnp.min(jnp.where(ms_ok, 1.0, 0.0)))

    def fast_group(g):
        for t in range(tps):
            fast_tile(g * tps + t, t)

    for g in range(nq // tps):
        pl.when(i == g)(functools.partial(fast_group, g))

    def online_tile(t):
        c = i * tps + t
        qq = qq_ref[pl.ds(pl.multiple_of(2 * t * tq, 2 * tq), 2 * tq), :]
        diag_off = pl.multiple_of(c * tq, tq)

        def update(carry, s, vt):
            m, l, acc = carry
            m_new = jnp.maximum(m, jnp.max(s, axis=-1, keepdims=True))
            alpha = jnp.exp2(m - m_new)
            p = jnp.exp2(s - m_new)
            l = alpha * l + jnp.sum(p, axis=-1, keepdims=True)
            acc = alpha * acc + pv(p.astype(BF16), vt)
            return m_new, l, acc

        def full_tile(j, carry):
            off = pl.multiple_of(j * tq, tq)
            return update(carry, scores(qq, k_ref[pl.ds(off, tq), :]),
                          v_ref[pl.ds(off, tq), :])

        s0 = jnp.where(lane < N_META, scores(qq, km_ref[...]), MASK_VALUE)
        m0 = jnp.max(s0, axis=-1, keepdims=True)
        p0 = jnp.exp2(s0 - m0)
        carry = (m0, jnp.sum(p0, axis=-1, keepdims=True), pv(p0.astype(BF16), vm_ref[...]))
        carry = lax.fori_loop(0, c, full_tile, carry)
        s = scores(qq, k_ref[pl.ds(diag_off, tq), :])
        row = lax.broadcasted_iota(jnp.int32, s.shape, 0)
        qpos = (row & (th - 1)) + jnp.where(row >= tq, th, 0)
        kpos = lax.broadcasted_iota(jnp.int32, s.shape, 1)
        s = jnp.where(kpos <= qpos, s, MASK_VALUE)
        _, l, acc = update(carry, s, v_ref[pl.ds(diag_off, tq), :])
        o_ref[pl.ds(pl.multiple_of(t * tq, tq), tq), :] = finish(l, acc)[0].astype(BF16)

    def check_slot(t, _):
        pl.when(ok_ref[t] < 0.5)(functools.partial(online_tile, t))
        return 0

    lax.fori_loop(0, tps, check_slot, 0)


def _attention(lamp, q, k, v, km, vm, g_subln, *, nb, seq, tq, tps):
    nq = seq // tq
    ng = nq // tps
    q_spec = pl.BlockSpec((tps * tq, V_DIM), lambda b, h, i: (b * ng + i, h))
    kv_spec = pl.BlockSpec((seq, V_DIM), lambda b, h, i: (b, h))
    meta_spec = pl.BlockSpec((LANES, V_DIM), lambda b, h, i: (0, h))
    return pl.pallas_call(
        functools.partial(_attn_kernel, tq=tq, nq=nq, tps=tps),
        grid=(nb, N_HEADS, ng),
        in_specs=[pl.BlockSpec((4, HEAD_DIM), lambda b, h, i: (0, 0)),
                  q_spec, kv_spec, kv_spec, meta_spec, meta_spec,
                  pl.BlockSpec((1, V_DIM), lambda b, h, i: (0, 0))],
        out_specs=q_spec,
        out_shape=jax.ShapeDtypeStruct(q.shape, BF16),
        scratch_shapes=[pltpu.VMEM((2 * tps * tq, LANES), BF16),
                        pltpu.SMEM((tps,), F32)],
        compiler_params=pltpu.CompilerParams(
            dimension_semantics=("arbitrary", "arbitrary", "arbitrary"),
            vmem_limit_bytes=VMEM_LIMIT),
        name="diffattn",
    )(lamp, q, k, v, km, vm, g_subln)


def _merge_kernel(x_ref, y_ref, o_ref, sgr_ref, sga_ref, wr_ref, wa_ref, wo_ref, h_ref):
    y_r = jnp.dot(y_ref[...], wr_ref[...], preferred_element_type=F32)
    y_a = jnp.dot(o_ref[...], wa_ref[...], preferred_element_type=F32)
    merged = sgr_ref[...].astype(F32) * y_r + sga_ref[...].astype(F32) * y_a
    h_ref[...] = x_ref[...] + jnp.dot(merged.astype(BF16), wo_ref[...],
                                      preferred_element_type=F32)


def _merge(x2d, y, o, sgr, sga, wr, wa, wo, *, tm):
    n = x2d.shape[0]
    row_spec = pl.BlockSpec((tm, D_MODEL), lambda i: (i, 0))
    w_spec = _resident((D_MODEL, D_MODEL))
    return pl.pallas_call(
        _merge_kernel,
        grid=(n // tm,),
        in_specs=[row_spec] * 5 + [w_spec] * 3,
        out_specs=row_spec,
        out_shape=jax.ShapeDtypeStruct((n, D_MODEL), F32),
        compiler_params=pltpu.CompilerParams(
            dimension_semantics=("arbitrary",), vmem_limit_bytes=VMEM_LIMIT),
        name="merge",
    )(x2d, y, o, sgr, sga, wr, wa, wo)


def _mlp_kernel(h_ref, g_ref, w1_ref, w2_ref, gf_ref, out_ref):
    h = h_ref[...]
    hn = (h * _rms_scale(h) * g_ref[...]).astype(BF16)
    acc = h
    for c in range(D_FF // D_MODEL):
        ff = jnp.dot(hn, w1_ref[:, c * D_MODEL:(c + 1) * D_MODEL],
                     preferred_element_type=F32)
        ff = jnp.square(jnp.maximum(ff, 0.0)).astype(BF16)
        acc = acc + jnp.dot(ff, w2_ref[c * D_MODEL:(c + 1) * D_MODEL, :],
                            preferred_element_type=F32)
    out_ref[...] = acc * _rms_scale(acc) * gf_ref[...]


def _mlp(h, g, w1, w2, gf, *, tm):
    n = h.shape[0]
    row_spec = pl.BlockSpec((tm, D_MODEL), lambda i: (i, 0))
    return pl.pallas_call(
        _mlp_kernel,
        grid=(n // tm,),
        in_specs=[row_spec, _resident((1, D_MODEL)), _resident((D_MODEL, D_FF)),
                  _resident((D_FF, D_MODEL)), _resident((1, D_MODEL))],
        out_specs=row_spec,
        out_shape=jax.ShapeDtypeStruct((n, D_MODEL), F32),
        compiler_params=pltpu.CompilerParams(
            dimension_semantics=("arbitrary",), vmem_limit_bytes=VMEM_LIMIT),
        name="mlp",
    )(h, g, w1, w2, gf)


def _rope_tables(n_pos):
    inv = 1.0 / (ROPE_THETA ** (jnp.arange(0, HEAD_DIM, 2, dtype=F32) / HEAD_DIM))
    ang = jnp.arange(n_pos, dtype=F32)[:, None] * inv[None, :]
    cos, sin = jnp.cos(ang), jnp.sin(ang)
    reps = LANES // HEAD_DIM
    return (jnp.tile(cos, (1, 2 * reps)),
            jnp.tile(jnp.concatenate([-sin, sin], axis=1), (1, reps)))


def kernel(x, meta_tokens, g_mix, w_in, conv_w, conv_b, w_a, b_a, w_x, b_x, lru_lambda,
           lam_q1, lam_k1, lam_q2, lam_k2, g_subln, w_rnn_out, w_attn_out, w_o,
           g_mlp, w_ff1, w_ff2, g_final):
    nb, seq, _ = x.shape
    tm = 512
    tc = 512
    tq = 512
    tps = math.gcd(seq // tq, 8)
    x2d = x.reshape(nb * seq, D_MODEL)
    row = lambda p: p.reshape(1, -1).astype(F32)

    cos, sin = _rope_tables(N_META + seq)
    q_scale = math.log2(math.e) / math.sqrt(HEAD_DIM)
    w_in_b = w_in[0].astype(BF16)
    g_mix_r = row(g_mix[0])

    inproj = functools.partial(_inproj, q_scale=q_scale)
    xr_m, gr_m, _, k_m, v_m, _, _ = inproj(
        meta_tokens.astype(F32), g_mix_r, w_in_b, cos[:N_META], sin[:N_META],
        tm=N_META, pos_tiles=1)
    xr, gr, q, k, v, sgr, sga = inproj(
        x2d, g_mix_r, w_in_b, cos[N_META:], sin[N_META:], tm=tm, pos_tiles=seq // tm)

    rnn_params = (conv_w[0].astype(F32), row(conv_b[0]), (0.5 * w_a[0]).astype(BF16),
                  row(0.5 * b_a[0]), (0.5 * w_x[0]).astype(BF16), row(0.5 * b_x[0]),
                  row(lru_lambda[0]))
    zeros = jnp.zeros((SUBLANES, D_MODEL), F32)
    _, h_meta, xtail_meta = _rglru(xr_m, gr_m, zeros, zeros, *rnn_params, nb=1, tc=N_META)
    y = _rglru_seg(xr, gr, h_meta[0], xtail_meta[0], *rnn_params, nb=nb, tc=tc)

    lamp = jnp.stack([lam_q1[0], lam_k1[0], lam_q2[0], lam_k2[0]]).astype(F32)
    pad = ((0, LANES - N_META), (0, 0))
    o = _attention(lamp, q, k, v, jnp.pad(k_m, pad), jnp.pad(v_m, pad), row(g_subln[0]),
                   nb=nb, seq=seq, tq=tq, tps=tps)

    h1 = _merge(x2d, y, o, sgr, sga, w_rnn_out[0].astype(BF16),
                w_attn_out[0].astype(BF16), w_o[0].astype(BF16), tm=tm)
    out = _mlp(h1, row(g_mlp[0]), w_ff1[0].astype(BF16), w_ff2[0].astype(BF16),
               row(g_final), tm=tm)
    return out.reshape(nb, seq, D_MODEL)
```
